```python
import math
import jax, jax.numpy as jnp
from jax import lax
import numpy as np

D_MODEL = 2048
BATCH = 1
SEQ = 16384
DEPTH = 4

MIX_WIDTH = D_MODEL
SSD_WIDTH = MIX_WIDTH // 2
SSD_HEAD_DIM = 64
SSD_HEADS = SSD_WIDTH // SSD_HEAD_DIM
SSD_GROUPS = 2
SSD_STATE = 128
SSD_CONV_WIDTH = 5
SSD_CHUNK = 128
CONF_CH = MIX_WIDTH - SSD_WIDTH
CONF_CONV_WIDTH = 31
XBC_WIDTH = SSD_WIDTH + 2 * SSD_GROUPS * SSD_STATE
IN_COLS = SSD_WIDTH + XBC_WIDTH + 2 * SSD_HEADS + 2 * CONF_CH
D_FF = 5632
N_EXPERTS = 8
TOP_K = 2
D_FF_EXPERT = 7168
EXPERT_BLOCK = 128
ALPHA = (2 * DEPTH) ** 0.25
BETA = (8 * DEPTH) ** -0.25
ADA_SCALE = 0.1
EPS = 1e-5
DT_MIN = 1e-3
DT_MAX = 1e-1

kernel_name = "hybrid_ssd_conformer_moe_deepnorm_encoder"


def layer_norm(v, g, b):
    v32 = v.astype(jnp.float32)
    mu = jnp.mean(v32, axis=-1, keepdims=True)
    var = jnp.mean(jnp.square(v32 - mu), axis=-1, keepdims=True)
    return ((v32 - mu) * lax.rsqrt(var + EPS) * g + b).astype(v.dtype)


def rms_norm(v, w, dtype):
    v32 = v.astype(jnp.float32)
    return (v32 * lax.rsqrt(jnp.mean(jnp.square(v32), axis=-1, keepdims=True) + EPS) * w).astype(dtype)


def depthwise_conv(v, w, b):
    k = w.shape[0]
    pad = (k - 1) // 2
    y = lax.conv_general_dilated(
        v, w[:, None, :].astype(v.dtype), window_strides=(1,), padding=[(pad, k - 1 - pad)],
        dimension_numbers=("NWC", "WIO", "NWC"), feature_group_count=v.shape[-1])
    return y + b


def ssd_scan(x, dt, a, bm, cm):
    b, l, h, p = x.shape
    g, n = bm.shape[-2:]
    r = h // g
    nc = l // SSD_CHUNK
    L = SSD_CHUNK
    xdt = (x.astype(jnp.float32) * dt[..., None]).reshape(b, nc, L, g, r, p)
    da = (dt * a).reshape(b, nc, L, g, r)
    bc = bm.astype(jnp.float32).reshape(b, nc, L, g, n)
    cc = cm.astype(jnp.float32).reshape(b, nc, L, g, n)
    cum = jnp.cumsum(da, axis=2)
    seg = cum[:, :, :, None] - cum[:, :, None, :]
    mask = jnp.tril(jnp.ones((L, L), dtype=bool))[:, :, None, None]
    decay = jnp.exp(jnp.where(mask, seg, -jnp.inf))
    cb = jnp.einsum("bclgn,bcsgn->bclsg", cc, bc)
    y_diag = jnp.einsum("bclsg,bclsgr,bcsgrp->bclgrp", cb, decay, xdt)
    decay_states = jnp.exp(cum[:, :, -1:] - cum)
    states = jnp.einsum("bclgn,bclgr,bclgrp->bcgrpn", bc, decay_states, xdt)
    chunk_decay = jnp.exp(cum[:, :, -1])

    def step(carry, inp):
        st, dec = inp
        return carry * dec[..., None, None] + st, carry

    init = jnp.zeros_like(states[:, 0])
    _, prev = lax.scan(step, init, (jnp.swapaxes(states, 0, 1), jnp.swapaxes(chunk_decay, 0, 1)))
    prev = jnp.swapaxes(prev, 0, 1)
    y_off = jnp.einsum("bclgn,bcgrpn,bclgr->bclgrp", cc, prev, jnp.exp(cum))
    return (y_diag + y_off).reshape(b, l, h, p)


def hybrid_mixer(h, w_in, ssd_conv_w, ssd_conv_b, dt_bias, a_log, d_skip, ssd_norm_w,
                 conf_conv_w, conf_conv_b, conf_ln_g, conf_ln_b, w_out):
    bsz, s, _ = h.shape
    f32 = jnp.float32
    proj = h @ w_in
    o1 = SSD_WIDTH
    o2 = o1 + XBC_WIDTH
    o3 = o2 + SSD_HEADS
    o4 = o3 + SSD_HEADS
    z, xbc, dt_f, dt_b, glu_in = jnp.split(proj, [o1, o2, o3, o4], axis=-1)
    xbc = jax.nn.silu(depthwise_conv(xbc, ssd_conv_w, ssd_conv_b))
    xs, bm, cm = jnp.split(xbc, [SSD_WIDTH, SSD_WIDTH + SSD_GROUPS * SSD_STATE], axis=-1)
    xs = xs.reshape(bsz, s, SSD_HEADS, SSD_HEAD_DIM)
    bm = bm.reshape(bsz, s, SSD_GROUPS, SSD_STATE)
    cm = cm.reshape(bsz, s, SSD_GROUPS, SSD_STATE)
    dtf = jax.nn.softplus(dt_f.astype(f32) + dt_bias[0].astype(f32))
    dtb = jax.nn.softplus(dt_b.astype(f32) + dt_bias[1].astype(f32))
    a_f = -jnp.exp(a_log[0].astype(f32))
    a_b = -jnp.exp(a_log[1].astype(f32))
    fl = lambda t: jnp.flip(t, axis=1)
    y = ssd_scan(xs, dtf, a_f, bm, cm) + fl(ssd_scan(fl(xs), fl(dtb), a_b, fl(bm), fl(cm)))
    y = y + d_skip.astype(f32)[:, None] * xs.astype(f32)
    y = y.reshape(bsz, s, SSD_WIDTH)
    y_ssd = rms_norm(y * jax.nn.silu(z.astype(f32)), ssd_norm_w.astype(f32), h.dtype)
    u = glu_in[..., :CONF_CH] * jax.nn.sigmoid(glu_in[..., CONF_CH:])
    u = depthwise_conv(u, conf_conv_w, conf_conv_b)
    u = jax.nn.silu(layer_norm(u, conf_ln_g, conf_ln_b))
    return jnp.concatenate([y_ssd, u.astype(h.dtype)], axis=-1) @ w_out


def swiglu(h, w_gate, w_up, w_down):
    return (jax.nn.silu(h @ w_gate) * (h @ w_up)) @ w_down


def moe_swiglu(h, w_router, w_gate, w_up, w_down):
    bsz, s, d = h.shape
    t = bsz * s
    hf = h.reshape(t, d)
    logits = (hf @ w_router).astype(jnp.float32)
    top_v, top_i = lax.top_k(logits, TOP_K)
    gates = jax.nn.softmax(top_v, axis=-1)
    n_assign = t * TOP_K
    flat_e = top_i.reshape(-1).astype(jnp.int32)
    flat_tok = jnp.arange(n_assign, dtype=jnp.int32) // TOP_K
    flat_w = gates.reshape(-1)
    order = jnp.argsort(flat_e)
    se = flat_e[order]
    counts = jnp.bincount(flat_e, length=N_EXPERTS).astype(jnp.int32)
    padded = (counts + EXPERT_BLOCK - 1) // EXPERT_BLOCK * EXPERT_BLOCK
    pad_end = jnp.cumsum(padded)
    pad_start = pad_end - padded
    start = jnp.cumsum(counts) - counts
    dest = pad_start[se] + (jnp.arange(n_assign, dtype=jnp.int32) - start[se])
    n_blocks = -(-n_assign // EXPERT_BLOCK) + N_EXPERTS
    p_tot = n_blocks * EXPERT_BLOCK
    tok_pad = jnp.full((p_tot,), t, dtype=jnp.int32).at[dest].set(flat_tok[order])
    w_pad = jnp.zeros((p_tot,), jnp.float32).at[dest].set(flat_w[order])
    block_expert = jnp.minimum(
        jnp.searchsorted(pad_end, jnp.arange(n_blocks, dtype=jnp.int32) * EXPERT_BLOCK, side="right"),
        N_EXPERTS - 1).astype(jnp.int32)
    h_ext = jnp.concatenate([hf, jnp.zeros((1, d), hf.dtype)], axis=0)
    xb = h_ext[tok_pad].reshape(n_blocks, EXPERT_BLOCK, d)

    def expert_block(args):
        blk, e = args
        return (jax.nn.silu(blk @ w_gate[e]) * (blk @ w_up[e])) @ w_down[e]

    yb = lax.map(expert_block, (xb, block_expert)).reshape(p_tot, d)
    y = jax.ops.segment_sum(yb * w_pad[:, None].astype(yb.dtype), tok_pad, num_segments=t + 1)[:t]
    return y.reshape(bsz, s, d)


def setup_inputs(seed: int = 0) -> dict:
    key = jax.random.key(seed)
    ks = iter(jax.random.split(key, 40))
    f32 = jnp.float32
    nrm = lambda shape, sc: jax.random.normal(next(ks), shape, f32) * sc
    n_dense = (DEPTH + 1) // 2
    n_moe = DEPTH // 2
    dt = jnp.exp(jax.random.uniform(next(ks), (DEPTH, 2, SSD_HEADS), f32,
                                    minval=math.log(DT_MIN), maxval=math.log(DT_MAX)))
    dt_bias = dt + jnp.log(-jnp.expm1(-dt))
    a_log = jnp.log(jax.random.uniform(next(ks), (DEPTH, 2, SSD_HEADS), f32, minval=1.0, maxval=16.0))
    return {
        "x": nrm((BATCH, SEQ, D_MODEL), 1.0),
        "c": nrm((BATCH, D_MODEL), 1.0),
        "w_ada": nrm((DEPTH, D_MODEL, 6 * D_MODEL), ADA_SCALE * D_MODEL ** -0.5),
        "b_ada": nrm((DEPTH, 6 * D_MODEL), 0.01),
        "w_in": nrm((DEPTH, D_MODEL, IN_COLS), D_MODEL ** -0.5),
        "ssd_conv_w": nrm((DEPTH, SSD_CONV_WIDTH, XBC_WIDTH), SSD_CONV_WIDTH ** -0.5),
        "ssd_conv_b": nrm((DEPTH, XBC_WIDTH), 0.01),
        "dt_bias": dt_bias,
        "a_log": a_log,
        "d_skip": 1.0 + nrm((DEPTH, SSD_HEADS), 0.1),
        "ssd_norm_w": 1.0 + nrm((DEPTH, SSD_WIDTH), 0.02),
        "conf_conv_w": nrm((DEPTH, CONF_CONV_WIDTH, CONF_CH), CONF_CONV_WIDTH ** -0.5),
        "conf_conv_b": nrm((DEPTH, CONF_CH), 0.01),
        "conf_ln_g": 1.0 + nrm((DEPTH, CONF_CH), 0.02),
        "conf_ln_b": nrm((DEPTH, CONF_CH), 0.01),
        "w_out": nrm((DEPTH, MIX_WIDTH, D_MODEL), BETA * MIX_WIDTH ** -0.5),
        "ln1_g": 1.0 + nrm((DEPTH, D_MODEL), 0.02),
        "ln1_b": nrm((DEPTH, D_MODEL), 0.01),
        "ln2_g": 1.0 + nrm((DEPTH, D_MODEL), 0.02),
        "ln2_b": nrm((DEPTH, D_MODEL), 0.01),
        "ffn_w_gate": nrm((n_dense, D_MODEL, D_FF), D_MODEL ** -0.5),
        "ffn_w_up": nrm((n_dense, D_MODEL, D_FF), D_MODEL ** -0.5),
        "ffn_w_down": nrm((n_dense, D_FF, D_MODEL), BETA * D_FF ** -0.5),
        "moe_router": nrm((n_moe, D_MODEL, N_EXPERTS), D_MODEL ** -0.5),
        "moe_w_gate": nrm((n_moe, N_EXPERTS, D_MODEL, D_FF_EXPERT), D_MODEL ** -0.5),
        "moe_w_up": nrm((n_moe, N_EXPERTS, D_MODEL, D_FF_EXPERT), D_MODEL ** -0.5),
        "moe_w_down": nrm((n_moe, N_EXPERTS, D_FF_EXPERT, D_MODEL), BETA * D_FF_EXPERT ** -0.5),
    }


def reference(x, c, w_ada, b_ada, w_in, ssd_conv_w, ssd_conv_b, dt_bias, a_log, d_skip, ssd_norm_w,
              conf_conv_w, conf_conv_b, conf_ln_g, conf_ln_b, w_out, ln1_g, ln1_b, ln2_g, ln2_b,
              ffn_w_gate, ffn_w_up, ffn_w_down, moe_router, moe_w_gate, moe_w_up, moe_w_down):
    c_act = jax.nn.silu(c)
    for l in range(DEPTH):
        mod = (c_act @ w_ada[l] + b_ada[l])[:, None, :]
        sh_m, sc_m, g_m, sh_f, sc_f, g_f = jnp.split(mod, 6, axis=-1)
        h = x * (1.0 + sc_m) + sh_m
        mix = hybrid_mixer(h, w_in[l], ssd_conv_w[l], ssd_conv_b[l], dt_bias[l], a_log[l], d_skip[l],
                           ssd_norm_w[l], conf_conv_w[l], conf_conv_b[l], conf_ln_g[l], conf_ln_b[l], w_out[l])
        x = layer_norm(ALPHA * x + (1.0 + g_m) * mix, ln1_g[l], ln1_b[l])
        h = x * (1.0 + sc_f) + sh_f
        if l % 2 == 0:
            i = l // 2
            f = swiglu(h, ffn_w_gate[i], ffn_w_up[i], ffn_w_down[i])
        else:
            i = l // 2
            f = moe_swiglu(h, moe_router[i], moe_w_gate[i], moe_w_up[i], moe_w_down[i])
        x = layer_norm(ALPHA * x + (1.0 + g_f) * f, ln2_g[l], ln2_b[l])
    return x
```

```python
import functools
import math

import jax
import jax.numpy as jnp
from jax import lax
from jax.experimental import pallas as pl
from jax.experimental.pallas import tpu as pltpu

F32 = jnp.float32
BF16 = jnp.bfloat16

SSD_HEADS = 16
SSD_HEAD_DIM = 64
SSD_WIDTH = SSD_HEADS * SSD_HEAD_DIM
SSD_GROUPS = 2
SSD_STATE = 128
SSD_CHUNK = 128
SSD_CONV_WIDTH = 5
CONF_CH = 1024
CONF_CONV_WIDTH = 31
N_EXPERTS = 8
EPS = 1e-5

COL_Z = 0
COL_XS = SSD_WIDTH
COL_BC = 2 * SSD_WIDTH
BC_WIDTH = 2 * SSD_GROUPS * SSD_STATE
COL_DT = COL_BC + BC_WIDTH
DT_PAD = 512
COL_GLU = COL_DT + DT_PAD
PROJ_COLS = COL_GLU + 2 * CONF_CH

LANE = 128
VMEM_LIMIT = 56 * 1024 * 1024


def _params(sem, vmem=VMEM_LIMIT):
    return pltpu.CompilerParams(dimension_semantics=sem, vmem_limit_bytes=vmem)


def _silu(v):
    return v * jax.nn.sigmoid(v)


def _softplus(v):
    return jnp.maximum(v, 0.0) + jnp.log1p(jnp.exp(-jnp.abs(v)))


def _layer_norm(v, g, b):
    mu = jnp.mean(v, axis=-1, keepdims=True)
    d = v - mu
    var = jnp.mean(d * d, axis=-1, keepdims=True)
    return d * lax.rsqrt(var + EPS) * g + b


def _ada_kernel(cb_ref, w_ref, b_ref, o_ref):
    cb = cb_ref[...]
    act = _silu(cb)
    tn = w_ref.shape[-1]
    parts = [jnp.sum(w_ref[:, j * LANE:(j + 1) * LANE] * act, axis=0, keepdims=True)
             for j in range(tn // LANE)]
    o_ref[...] = jnp.concatenate(parts, axis=1) + b_ref[...]


def _ada_mod(c, w_ada, b_ada):
    depth, d, n = w_ada.shape
    tn = 1024
    cb = jnp.broadcast_to(c.reshape(d, 1), (d, LANE))
    return pl.pallas_call(
        _ada_kernel,
        grid=(depth, n // tn),
        in_specs=[pl.BlockSpec((d, LANE), lambda l, j: (0, 0)),
                  pl.BlockSpec((None, d, tn), lambda l, j: (l, 0, j)),
                  pl.BlockSpec((None, 1, tn), lambda l, j: (l, 0, j))],
        out_specs=pl.BlockSpec((None, 1, tn), lambda l, j: (l, 0, j)),
        out_shape=jax.ShapeDtypeStruct((depth, 1, n), F32),
        compiler_params=_params(("parallel", "parallel")),
        name="ada_mod",
    )(cb, w_ada, b_ada.reshape(depth, 1, n))


def _mod_matmul_kernel(x_ref, sc_ref, sh_ref, w_ref, o_ref, h_ref):
    @pl.when(pl.program_id(1) == 0)
    def _():
        h_ref[...] = (x_ref[...] * (1.0 + sc_ref[...]) + sh_ref[...]).astype(BF16)

    o_ref[...] = jnp.dot(h_ref[...], w_ref[...], preferred_element_type=F32)


def _mod_matmul(x, sc, sh, w):
    s, d = x.shape
    n = w.shape[1]
    tm = min(1024, s)
    tn = 1024
    return pl.pallas_call(
        _mod_matmul_kernel,
        grid=(s // tm, n // tn),
        in_specs=[pl.BlockSpec((tm, d), lambda i, j: (i, 0)),
                  pl.BlockSpec((1, d), lambda i, j: (0, 0)),
                  pl.BlockSpec((1, d), lambda i, j: (0, 0)),
                  pl.BlockSpec((d, tn), lambda i, j: (0, j))],
        out_specs=pl.BlockSpec((tm, tn), lambda i, j: (i, j)),
        out_shape=jax.ShapeDtypeStruct((s, n), F32),
        scratch_shapes=[pltpu.VMEM((tm, d), BF16)],
        compiler_params=_params(("parallel", "arbitrary")),
        name="in_proj",
    )(x, sc, sh, w)


def _fill_halo(ext_ref, prev_ref, x_ref, next_ref, halo):
    i = pl.program_id(0)
    t = x_ref.shape[0]
    ext_ref[halo:halo + t, :] = x_ref[...]
    ext_ref[0:halo, :] = jnp.where(i > 0, prev_ref[...], 0.0)
    ext_ref[halo + t:halo + t + halo, :] = jnp.where(i < pl.num_programs(0) - 1, next_ref[...], 0.0)


def _dwconv_rows(ext_ref, w_ref, b_ref, halo, r0, rows, c0, cols):
    k = w_ref.shape[0]
    pad = (k - 1) // 2
    acc = jnp.broadcast_to(b_ref[:, c0:c0 + cols], (rows, cols))
    for tap in range(k):
        start = halo - pad + tap + r0
        acc = acc + w_ref[tap:tap + 1, c0:c0 + cols] * ext_ref[start:start + rows, c0:c0 + cols]
    return acc


SSD_HALO = 8
CONV_ROWS = 32


def _ssd_conv_kernel(prev_ref, x_ref, next_ref, w_ref, b_ref, o_ref, ext_ref):
    _fill_halo(ext_ref, prev_ref, x_ref, next_ref, SSD_HALO)
    t, c = x_ref.shape
    for r0 in range(0, t, CONV_ROWS):
        for c0 in range(0, c, 256):
            acc = _dwconv_rows(ext_ref, w_ref, b_ref, SSD_HALO, r0, CONV_ROWS, c0, 256)
            o_ref[r0:r0 + CONV_ROWS, c0:c0 + 256] = _silu(acc)


def _ssd_conv(proj, w, b):
    s = proj.shape[0]
    t = min(256, s)
    cw = 512
    nc = (SSD_WIDTH + BC_WIDTH) // cw
    c_off = COL_XS // cw
    hb = t // SSD_HALO
    last = s // SSD_HALO - 1
    return pl.pallas_call(
        _ssd_conv_kernel,
        grid=(s // t, nc),
        in_specs=[pl.BlockSpec((SSD_HALO, cw), lambda i, c: (jnp.maximum(i * hb - 1, 0), c + c_off)),
                  pl.BlockSpec((t, cw), lambda i, c: (i, c + c_off)),
                  pl.BlockSpec((SSD_HALO, cw), lambda i, c: (jnp.minimum((i + 1) * hb, last), c + c_off)),
                  pl.BlockSpec((SSD_CONV_WIDTH, cw), lambda i, c: (0, c)),
                  pl.BlockSpec((1, cw), lambda i, c: (0, c))],
        out_specs=pl.BlockSpec((t, cw), lambda i, c: (i, c)),
        out_shape=jax.ShapeDtypeStruct((s, SSD_WIDTH + BC_WIDTH), F32),
        scratch_shapes=[pltpu.VMEM((t + 2 * SSD_HALO, cw), F32)],
        compiler_params=_params(("parallel", "parallel")),
        name="ssd_conv",
    )(proj, proj, proj, w, b.reshape(1, -1))


def _ssd_scan_kernel(xs_ref, bc_ref, dt_ref, dtb_ref, alog_ref, exp_ref, o_ref, state_ref, *, reverse):
    L = SSD_CHUNK
    gw = SSD_WIDTH // SSD_GROUPS
    hp = lax.Precision.HIGHEST

    @pl.when(pl.program_id(0) == 0)
    def _():
        state_ref[...] = jnp.zeros_like(state_ref)

    row = lax.broadcasted_iota(jnp.int32, (L, L), 0)
    col = lax.broadcasted_iota(jnp.int32, (L, L), 1)
    mask = (row <= col) if reverse else (row >= col)
    tri = mask.astype(F32)
    edge = 0 if reverse else L - 1

    dt_off = SSD_HEADS if reverse else 0
    dt = _softplus(dt_ref[:, dt_off:dt_off + SSD_HEADS] + dtb_ref[...])
    a = -jnp.exp(alog_ref[...])
    da = dt * a
    cum = jnp.dot(tri, da, precision=hp, preferred_element_type=F32)
    cum_t = lax.dot_general(da, tri, (((0,), (1,)), ((), ())), precision=hp,
                            preferred_element_type=F32)
    total = cum[edge:edge + 1, :]
    expand = exp_ref[...]
    dt_x = jnp.dot(dt, expand, precision=hp, preferred_element_type=F32)
    ecum_x = jnp.dot(jnp.exp(cum), expand, precision=hp, preferred_element_type=F32)
    dstate_x = jnp.dot(jnp.exp(total - cum), expand, precision=hp, preferred_element_type=F32)
    cdecay_x = ecum_x[edge:edge + 1, :]

    xdt = xs_ref[...] * dt_x
    xdt_b = xdt.astype(BF16)
    xdt_s = (xdt * dstate_x).astype(BF16)

    for g in range(SSD_GROUPS):
        bg = bc_ref[:, g * SSD_STATE:(g + 1) * SSD_STATE].astype(BF16)
        cg = bc_ref[:, (SSD_GROUPS + g) * SSD_STATE:(SSD_GROUPS + g + 1) * SSD_STATE].astype(BF16)
        cb = lax.dot_general(cg, bg, (((1,), (1,)), ((), ())), preferred_element_type=F32)
        prev = state_ref[g]
        y_off = jnp.dot(cg, prev.astype(BF16), preferred_element_type=F32)
        y_off = y_off * ecum_x[:, g * gw:(g + 1) * gw]
        heads = []
        for r in range(SSD_HEADS // SSD_GROUPS):
            h = g * (SSD_HEADS // SSD_GROUPS) + r
            seg = cum[:, h:h + 1] - cum_t[h:h + 1, :]
            m = (cb * jnp.exp(jnp.where(mask, seg, -jnp.inf))).astype(BF16)
            heads.append(jnp.dot(m, xdt_b[:, h * SSD_HEAD_DIM:(h + 1) * SSD_HEAD_DIM],
                                 preferred_element_type=F32))
        o_ref[:, g * gw:(g + 1) * gw] = jnp.concatenate(heads, axis=1) + y_off
        new = lax.dot_general(bg, xdt_s[:, g * gw:(g + 1) * gw], (((0,), (0,)), ((), ())),
                              preferred_element_type=F32)
        state_ref[g] = prev * cdecay_x[:, g * gw:(g + 1) * gw] + new


def _ssd_scan(xbc, proj, dt_bias, a_log, expand, reverse):
    s = xbc.shape[0]
    nc = s // SSD_CHUNK
    idx = (lambda i: nc - 1 - i) if reverse else (lambda i: i)
    bc_blk = SSD_WIDTH // BC_WIDTH
    dt_blk = COL_DT // LANE
    return pl.pallas_call(
        functools.partial(_ssd_scan_kernel, reverse=reverse),
        grid=(nc,),
        in_specs=[pl.BlockSpec((SSD_CHUNK, SSD_WIDTH), lambda i: (idx(i), 0)),
                  pl.BlockSpec((SSD_CHUNK, BC_WIDTH), lambda i: (idx(i), bc_blk)),
                  pl.BlockSpec((SSD_CHUNK, LANE), lambda i: (idx(i), dt_blk)),
                  pl.BlockSpec((1, SSD_HEADS), lambda i: (0, 0)),
                  pl.BlockSpec((1, SSD_HEADS), lambda i: (0, 0)),
                  pl.BlockSpec((SSD_HEADS, SSD_WIDTH), lambda i: (0, 0))],
        out_specs=pl.BlockSpec((SSD_CHUNK, SSD_WIDTH), lambda i: (idx(i), 0)),
        out_shape=jax.ShapeDtypeStruct((s, SSD_WIDTH), F32),
        scratch_shapes=[pltpu.VMEM((SSD_GROUPS, SSD_STATE, SSD_WIDTH // SSD_GROUPS), F32)],
        compiler_params=_params(("arbitrary",)),
        name="ssd_scan_bwd" if reverse else "ssd_scan_fwd",
    )(xbc, xbc, proj, dt_bias.reshape(1, -1), a_log.reshape(1, -1), expand)


def _ssd_out_kernel(yf_ref, yb_ref, xs_ref, z_ref, d_ref, w_ref, o_ref):
    y = yf_ref[...] + yb_ref[...] + d_ref[...] * xs_ref[...]
    v = y * _silu(z_ref[...])
    ms = jnp.mean(v * v, axis=-1, keepdims=True)
    o_ref[...] = (v * lax.rsqrt(ms + EPS) * w_ref[...]).astype(o_ref.dtype)


def _ssd_out(yf, yb, xbc, proj, d_x, norm_w):
    s = yf.shape[0]
    t = min(512, s)
    row = lambda i: (i, 0)
    fix = lambda i: (0, 0)
    return pl.pallas_call(
        _ssd_out_kernel,
        grid=(s // t,),
        in_specs=[pl.BlockSpec((t, SSD_WIDTH), row), pl.BlockSpec((t, SSD_WIDTH), row),
                  pl.BlockSpec((t, SSD_WIDTH), row),
                  pl.BlockSpec((t, SSD_WIDTH), row),
                  pl.BlockSpec((1, SSD_WIDTH), fix), pl.BlockSpec((1, SSD_WIDTH), fix)],
        out_specs=pl.BlockSpec((t, SSD_WIDTH), row),
        out_shape=jax.ShapeDtypeStruct((s, SSD_WIDTH), BF16),
        compiler_params=_params(("parallel",)),
        name="ssd_out",
    )(yf, yb, xbc, proj, d_x, norm_w.reshape(1, -1))


CONF_HALO = 16


def _conf_kernel(ap_ref, a_ref, an_ref, bp_ref, b_ref, bn_ref, w_ref, cb_ref, g_ref, beta_ref,
                 o_ref, ext_ref, gate_ref, conv_ref):
    _fill_halo(ext_ref, ap_ref, a_ref, an_ref, CONF_HALO)
    _fill_halo(gate_ref, bp_ref, b_ref, bn_ref, CONF_HALO)
    ext_ref[...] = ext_ref[...] * jax.nn.sigmoid(gate_ref[...])
    t, c = a_ref.shape
    for r0 in range(0, t, CONV_ROWS):
        for c0 in range(0, c, 256):
            conv_ref[r0:r0 + CONV_ROWS, c0:c0 + 256] = _dwconv_rows(
                ext_ref, w_ref, cb_ref, CONF_HALO, r0, CONV_ROWS, c0, 256)
    o_ref[...] = _silu(_layer_norm(conv_ref[...], g_ref[...], beta_ref[...])).astype(o_ref.dtype)


def _conformer(proj, conv_w, conv_b, ln_g, ln_b):
    s = proj.shape[0]
    t = min(256, s)
    hb = t // CONF_HALO
    last = s // CONF_HALO - 1
    ca = COL_GLU // CONF_CH
    cg = ca + 1
    prev = lambda col: (lambda i: (jnp.maximum(i * hb - 1, 0), col))
    cur = lambda col: (lambda i: (i, col))
    nxt = lambda col: (lambda i: (jnp.minimum((i + 1) * hb, last), col))
    fix = lambda i: (0, 0)
    vec = pl.BlockSpec((1, CONF_CH), fix)
    return pl.pallas_call(
        _conf_kernel,
        grid=(s // t,),
        in_specs=[pl.BlockSpec((CONF_HALO, CONF_CH), prev(ca)), pl.BlockSpec((t, CONF_CH), cur(ca)),
                  pl.BlockSpec((CONF_HALO, CONF_CH), nxt(ca)),
                  pl.BlockSpec((CONF_HALO, CONF_CH), prev(cg)), pl.BlockSpec((t, CONF_CH), cur(cg)),
                  pl.BlockSpec((CONF_HALO, CONF_CH), nxt(cg)),
                  pl.BlockSpec((CONF_CONV_WIDTH, CONF_CH), fix), vec, vec, vec],
        out_specs=pl.BlockSpec((t, CONF_CH), lambda i: (i, 0)),
        out_shape=jax.ShapeDtypeStruct((s, CONF_CH), BF16),
        scratch_shapes=[pltpu.VMEM((t + 2 * CONF_HALO, CONF_CH), F32),
                        pltpu.VMEM((t + 2 * CONF_HALO, CONF_CH), F32),
                        pltpu.VMEM((t, CONF_CH), F32)],
        compiler_params=_params(("parallel",)),
        name="conformer",
    )(proj, proj, proj, proj, proj, proj, conv_w, conv_b.reshape(1, -1), ln_g.reshape(1, -1),
      ln_b.reshape(1, -1))


def _out_proj_kernel(y_ref, u_ref, wy_ref, wu_ref, x_ref, gate_ref, g_ref, b_ref, sc_ref, sh_ref,
                     xo_ref, h_ref, *, alpha):
    mix = jnp.dot(y_ref[...], wy_ref[...], preferred_element_type=F32)
    mix = mix + jnp.dot(u_ref[...], wu_ref[...], preferred_element_type=F32)
    xn = _layer_norm(alpha * x_ref[...] + (1.0 + gate_ref[...]) * mix, g_ref[...], b_ref[...])
    xo_ref[...] = xn
    h_ref[...] = (xn * (1.0 + sc_ref[...]) + sh_ref[...]).astype(h_ref.dtype)


def _out_proj(y_ssd, u, w_out, x, gate, ln_g, ln_b, sc, sh, alpha):
    s, d = x.shape
    tm = min(512, s)
    half = w_out.shape[0] // 2
    row = lambda i: (i, 0)
    fix = lambda i: (0, 0)
    vec = pl.BlockSpec((1, d), fix)
    return pl.pallas_call(
        functools.partial(_out_proj_kernel, alpha=alpha),
        grid=(s // tm,),
        in_specs=[pl.BlockSpec((tm, half), row), pl.BlockSpec((tm, half), row),
                  pl.BlockSpec((half, d), lambda i: (0, 0)), pl.BlockSpec((half, d), lambda i: (1, 0)),
                  pl.BlockSpec((tm, d), row), vec, vec, vec, vec, vec],
        out_specs=[pl.BlockSpec((tm, d), row), pl.BlockSpec((tm, d), row)],
        out_shape=[jax.ShapeDtypeStruct((s, d), F32), jax.ShapeDtypeStruct((s, d), BF16)],
        compiler_params=_params(("parallel",)),
        name="out_proj",
    )(y_ssd, u, w_out, w_out, x, gate, ln_g.reshape(1, -1), ln_b.reshape(1, -1), sc, sh)


def _ffn_kernel(h_ref, wg_ref, wu_ref, wd_ref, x_ref, gate_ref, g_ref, b_ref, o_ref, acc_ref, *, alpha):
    j = pl.program_id(1)

    @pl.when(j == 0)
    def _():
        acc_ref[...] = jnp.zeros_like(acc_ref)

    h = h_ref[...]
    a = jnp.dot(h, wg_ref[...], preferred_element_type=F32)
    b = jnp.dot(h, wu_ref[...], preferred_element_type=F32)
    acc_ref[...] += jnp.dot((_silu(a) * b).astype(BF16), wd_ref[...], preferred_element_type=F32)

    @pl.when(j == pl.num_programs(1) - 1)
    def _():
        o_ref[...] = _layer_norm(alpha * x_ref[...] + (1.0 + gate_ref[...]) * acc_ref[...],
                                 g_ref[...], b_ref[...])


def _dense_ffn(h, wg, wu, wd, x, gate, ln_g, ln_b, alpha):
    s, d = x.shape
    f = wg.shape[1]
    tm = min(512, s)
    tf = 512
    row = lambda i, j: (i, 0)
    vec = pl.BlockSpec((1, d), lambda i, j: (0, 0))
    return pl.pallas_call(
        functools.partial(_ffn_kernel, alpha=alpha),
        grid=(s // tm, f // tf),
        in_specs=[pl.BlockSpec((tm, d), row),
                  pl.BlockSpec((d, tf), lambda i, j: (0, j)), pl.BlockSpec((d, tf), lambda i, j: (0, j)),
                  pl.BlockSpec((tf, d), lambda i, j: (j, 0)),
                  pl.BlockSpec((tm, d), row), vec, vec, vec],
        out_specs=pl.BlockSpec((tm, d), row),
        out_shape=jax.ShapeDtypeStruct((s, d), F32),
        scratch_shapes=[pltpu.VMEM((tm, d), F32)],
        compiler_params=_params(("parallel", "arbitrary")),
        name="dense_ffn",
    )(h, wg, wu, wd, x, gate, ln_g.reshape(1, -1), ln_b.reshape(1, -1))


def _router_kernel(x_ref, sc_ref, sh_ref, wr_ref, route_ref, cnt_ref, carry_ref):
    i = pl.program_id(0)
    tb = x_ref.shape[0]
    e = N_EXPERTS

    @pl.when(i == 0)
    def _():
        carry_ref[...] = jnp.zeros_like(carry_ref)

    h = x_ref[...] * (1.0 + sc_ref[...]) + sh_ref[...]
    logits = lax.dot_general(wr_ref[...], h, (((1,), (1,)), ((), ())), precision=lax.Precision.HIGHEST,
                             preferred_element_type=F32)
    eid = lax.broadcasted_iota(jnp.int32, (e, tb), 0).astype(F32)
    m1 = jnp.max(logits, axis=0, keepdims=True)
    i1 = jnp.min(jnp.where(logits == m1, eid, float(e)), axis=0, keepdims=True)
    rest = jnp.where(eid == i1, -jnp.inf, logits)
    m2 = jnp.max(rest, axis=0, keepdims=True)
    i2 = jnp.min(jnp.where(rest == m2, eid, float(e)), axis=0, keepdims=True)
    p2 = jnp.exp(m2 - m1)
    g1 = 1.0 / (1.0 + p2)
    g2 = p2 / (1.0 + p2)
    sel1 = eid == i1
    sel2 = eid == i2
    onehot = jnp.where(sel1 | sel2, 1.0, 0.0)
    r_i = lax.broadcasted_iota(jnp.int32, (tb, tb), 0)
    c_i = lax.broadcasted_iota(jnp.int32, (tb, tb), 1)
    before = (r_i < c_i).astype(BF16)
    rank = jnp.dot(onehot.astype(BF16), before, preferred_element_type=F32) + carry_ref[:, 0:1]
    r1 = jnp.sum(jnp.where(sel1, rank, 0.0), axis=0, keepdims=True)
    r2 = jnp.sum(jnp.where(sel2, rank, 0.0), axis=0, keepdims=True)
    zero = jnp.zeros_like(g1)
    route_ref[...] = jnp.concatenate(
        [i1, i2, r1, r2, g1, g2, zero, zero], axis=0)
    carry_ref[...] = carry_ref[...] + jnp.sum(onehot, axis=1, keepdims=True)
    cnt_ref[...] = carry_ref[...]


def _router(x, sc, sh, w_router_t):
    s, d = x.shape
    tb = min(512, s)
    fix = lambda i: (0, 0)
    return pl.pallas_call(
        _router_kernel,
        grid=(s // tb,),
        in_specs=[pl.BlockSpec((tb, d), lambda i: (i, 0)), pl.BlockSpec((1, d), fix),
                  pl.BlockSpec((1, d), fix), pl.BlockSpec((N_EXPERTS, d), fix)],
        out_specs=[pl.BlockSpec((8, tb), lambda i: (0, i)), pl.BlockSpec((N_EXPERTS, LANE), fix)],
        out_shape=[jax.ShapeDtypeStruct((8, s), F32), jax.ShapeDtypeStruct((N_EXPERTS, LANE), F32)],
        scratch_shapes=[pltpu.VMEM((N_EXPERTS, LANE), F32)],
        compiler_params=_params(("arbitrary",)),
        name="moe_router",
    )(x, sc, sh, w_router_t)


def _moe_kernel(be_ref, valid_ref, xb_ref, wg_ref, wu_ref, wd_ref, o_ref, acc_ref):
    b = pl.program_id(0)
    j = pl.program_id(1)
    last = pl.num_programs(1) - 1

    @pl.when(valid_ref[b] == 1)
    def _():
        @pl.when(j == 0)
        def _():
            acc_ref[...] = jnp.zeros_like(acc_ref)

        xb = xb_ref[...]
        a = jnp.dot(xb, wg_ref[...], preferred_element_type=F32)
        u = jnp.dot(xb, wu_ref[...], preferred_element_type=F32)
        acc_ref[...] += jnp.dot((_silu(a) * u).astype(BF16), wd_ref[...], preferred_element_type=F32)

        @pl.when(j == last)
        def _():
            o_ref[...] = acc_ref[...]

    @pl.when((valid_ref[b] == 0) & (j == last))
    def _():
        o_ref[...] = jnp.zeros_like(o_ref)


def _moe_experts(xb, block_expert, block_valid, wg, wu, wd, tb):
    p, d = xb.shape
    f = wg.shape[2]
    tf = 512
    nf = f // tf
    jj = lambda b, j, valid: jnp.where(valid[b] == 1, j, nf - 1)
    return pl.pallas_call(
        _moe_kernel,
        grid_spec=pltpu.PrefetchScalarGridSpec(
            num_scalar_prefetch=2,
            grid=(p // tb, nf),
            in_specs=[pl.BlockSpec((tb, d), lambda b, j, be, valid: (b, 0)),
                      pl.BlockSpec((None, d, tf), lambda b, j, be, valid: (be[b], 0, jj(b, j, valid))),
                      pl.BlockSpec((None, d, tf), lambda b, j, be, valid: (be[b], 0, jj(b, j, valid))),
                      pl.BlockSpec((None, tf, d), lambda b, j, be, valid: (be[b], jj(b, j, valid), 0))],
            out_specs=pl.BlockSpec((tb, d), lambda b, j, be, valid: (b, 0)),
            scratch_shapes=[pltpu.VMEM((tb, d), F32)]),
        out_shape=jax.ShapeDtypeStruct((p, d), F32),
        compiler_params=_params(("arbitrary", "arbitrary")),
        name="moe_experts",
    )(block_expert, block_valid, xb, wg, wu, wd)


def _res_ln_kernel(x_ref, f_ref, gate_ref, g_ref, b_ref, o_ref, *, alpha):
    o_ref[...] = _layer_norm(alpha * x_ref[...] + (1.0 + gate_ref[...]) * f_ref[...], g_ref[...], b_ref[...])


def _res_ln(x, f, gate, ln_g, ln_b, alpha):
    s, d = x.shape
    t = min(512, s)
    row = lambda i: (i, 0)
    vec = pl.BlockSpec((1, d), lambda i: (0, 0))
    return pl.pallas_call(
        functools.partial(_res_ln_kernel, alpha=alpha),
        grid=(s // t,),
        in_specs=[pl.BlockSpec((t, d), row), pl.BlockSpec((t, d), row), vec, vec, vec],
        out_specs=pl.BlockSpec((t, d), row),
        out_shape=jax.ShapeDtypeStruct((s, d), F32),
        compiler_params=_params(("parallel",)),
        name="res_ln",
    )(x, f, gate, ln_g.reshape(1, -1), ln_b.reshape(1, -1))


MOE_BLOCK = 512


def _moe_ffn(x, h, sc, sh, w_router, wg, wu, wd, gate, ln_g, ln_b, alpha):
    s, d = x.shape
    tb = min(MOE_BLOCK, s)
    route, counts = _router(x, sc, sh, w_router.T)
    e1 = route[0].astype(jnp.int32)
    e2 = route[1].astype(jnp.int32)
    counts = counts[:, 0].astype(jnp.int32)
    padded = (counts + tb - 1) // tb * tb
    pad_end = jnp.cumsum(padded)
    pad_start = pad_end - padded
    d1 = pad_start[e1] + route[2].astype(jnp.int32)
    d2 = pad_start[e2] + route[3].astype(jnp.int32)
    n_blocks = (2 * s) // tb + N_EXPERTS
    p_tot = n_blocks * tb
    tok = jnp.arange(s, dtype=jnp.int32)
    tok_pad = jnp.zeros((p_tot,), jnp.int32).at[d1].set(tok).at[d2].set(tok)
    blk_start = jnp.arange(n_blocks, dtype=jnp.int32) * tb
    block_valid = (blk_start < pad_end[-1]).astype(jnp.int32)
    block_expert = jnp.minimum(jnp.searchsorted(pad_end, blk_start, side="right"), N_EXPERTS - 1)
    last_expert = block_expert[jnp.maximum(pad_end[-1] // tb - 1, 0)]
    block_expert = jnp.where(block_valid == 1, block_expert, last_expert).astype(jnp.int32)
    xb = jnp.take(h, tok_pad, axis=0)
    yb = _moe_experts(xb, block_expert, block_valid, wg, wu, wd, tb)
    y = route[4][:, None] * jnp.take(yb, d1, axis=0) + route[5][:, None] * jnp.take(yb, d2, axis=0)
    return _res_ln(x, y, gate, ln_g, ln_b, alpha)


def kernel(x, c, w_ada, b_ada, w_in, ssd_conv_w, ssd_conv_b, dt_bias, a_log, d_skip, ssd_norm_w, conf_conv_w, conf_conv_b, conf_ln_g, conf_ln_b, w_out, ln1_g, ln1_b, ln2_g, ln2_b, ffn_w_gate, ffn_w_up, ffn_w_down, moe_router, moe_w_gate, moe_w_up, moe_w_down):
    bsz, s, d = x.shape
    assert bsz == 1
    depth = w_ada.shape[0]
    alpha = (2 * depth) ** 0.25
    xs = x.reshape(s, d)

    mod = _ada_mod(c, w_ada, b_ada)
    dt_end = COL_DT + 2 * SSD_HEADS
    w_proj = jnp.concatenate(
        [w_in[:, :, :dt_end].astype(BF16),
         jnp.zeros((depth, d, DT_PAD - 2 * SSD_HEADS), BF16),
         w_in[:, :, dt_end:].astype(BF16)], axis=2)
    w_out_b = w_out.astype(BF16)
    expand = jnp.repeat(jnp.eye(SSD_HEADS, dtype=F32), SSD_HEAD_DIM, axis=1)
    d_x = jnp.repeat(d_skip, SSD_HEAD_DIM, axis=1)

    for l in range(depth):
        sh_m, sc_m, g_m, sh_f, sc_f, g_f = [mod[l, :, k * d:(k + 1) * d] for k in range(6)]
        proj = _mod_matmul(xs, sc_m, sh_m, w_proj[l])
        xbc = _ssd_conv(proj, ssd_conv_w[l], ssd_conv_b[l])
        yf = _ssd_scan(xbc, proj, dt_bias[l, 0], a_log[l, 0], expand, reverse=False)
        yb = _ssd_scan(xbc, proj, dt_bias[l, 1], a_log[l, 1], expand, reverse=True)
        y_ssd = _ssd_out(yf, yb, xbc, proj, d_x[l:l + 1], ssd_norm_w[l])
        u = _conformer(proj, conf_conv_w[l], conf_conv_b[l], conf_ln_g[l], conf_ln_b[l])
        xs, h = _out_proj(y_ssd, u, w_out_b[l], xs, g_m, ln1_g[l], ln1_b[l], sc_f, sh_f, alpha)
        i = l // 2
        if l % 2 == 0:
            xs = _dense_ffn(h, ffn_w_gate[i].astype(BF16), ffn_w_up[i].astype(BF16),
                            ffn_w_down[i].astype(BF16), xs, g_f, ln2_g[l], ln2_b[l], alpha)
        else:
            xs = _moe_ffn(xs, h, sc_f, sh_f, moe_router[i], moe_w_gate[i].astype(BF16),
                          moe_w_up[i].astype(BF16), moe_w_down[i].astype(BF16), g_f,
                          ln2_g[l], ln2_b[l], alpha)
    return xs.reshape(bsz, s, d)
```

```python
import functools

import jax
import jax.numpy as jnp
from jax import lax
from jax.experimental import pallas as pl
from jax.experimental.pallas import tpu as pltpu

F32 = jnp.float32
BF16 = jnp.bfloat16

SSD_HEADS = 16
SSD_HEAD_DIM = 64
SSD_WIDTH = SSD_HEADS * SSD_HEAD_DIM
SSD_GROUPS = 2
SSD_STATE = 128
SSD_CHUNK = 128
SSD_CONV_WIDTH = 5
CONF_CH = 1024
CONF_CONV_WIDTH = 31
N_EXPERTS = 8
EPS = 1e-5

BC_WIDTH = 2 * SSD_GROUPS * SSD_STATE
XBC_WIDTH = SSD_WIDTH + BC_WIDTH
COL_DT = XBC_WIDTH
DT_PAD = 512
COL_Z = COL_DT + DT_PAD
COL_GLU = COL_Z + SSD_WIDTH
PROJ_COLS = COL_GLU + 2 * CONF_CH

LANE = 128
SUBLANE = 8
VMEM_LIMIT = 56 * 1024 * 1024


def _params(sem, vmem=VMEM_LIMIT):
    return pltpu.CompilerParams(dimension_semantics=sem, vmem_limit_bytes=vmem)


def _silu(v):
    return v * jax.nn.sigmoid(v)


def _softplus(v):
    return jnp.maximum(v, 0.0) + jnp.log1p(jnp.exp(-jnp.abs(v)))


def _layer_norm(v, g, b):
    mu = jnp.mean(v, axis=-1, keepdims=True)
    d = v - mu
    var = jnp.mean(d * d, axis=-1, keepdims=True)
    return d * lax.rsqrt(var + EPS) * g + b


def _split_bf16(v, terms):
    out = []
    for _ in range(terms):
        t = v.astype(BF16)
        out.append(t)
        v = v - t.astype(F32)
    return out


def _ada_kernel(cb_ref, w_ref, b_ref, o_ref):
    cb = cb_ref[...]
    act = _silu(cb)
    tn = w_ref.shape[-1]
    parts = [jnp.sum(w_ref[:, j * LANE:(j + 1) * LANE] * act, axis=0, keepdims=True)
             for j in range(tn // LANE)]
    o_ref[...] = jnp.concatenate(parts, axis=1) + b_ref[...]


def _ada_mod(c, w_ada, b_ada):
    depth, d, n = w_ada.shape
    tn = 1024
    cb = jnp.broadcast_to(c.reshape(d, 1), (d, LANE))
    return pl.pallas_call(
        _ada_kernel,
        grid=(depth, n // tn),
        in_specs=[pl.BlockSpec((d, LANE), lambda l, j: (0, 0)),
                  pl.BlockSpec((None, d, tn), lambda l, j: (l, 0, j)),
                  pl.BlockSpec((None, 1, tn), lambda l, j: (l, 0, j))],
        out_specs=pl.BlockSpec((None, 1, tn), lambda l, j: (l, 0, j)),
        out_shape=jax.ShapeDtypeStruct((depth, 1, n), F32),
        compiler_params=_params(("parallel", "parallel")),
        name="ada_mod",
    )(cb, w_ada, b_ada.reshape(depth, 1, n))


def _mod_matmul_kernel(x_ref, sc_ref, sh_ref, w_ref, o_ref, h_ref):
    @pl.when(pl.program_id(1) == 0)
    def _():
        h_ref[...] = (x_ref[...] * (1.0 + sc_ref[...]) + sh_ref[...]).astype(BF16)

    o_ref[...] = jnp.dot(h_ref[...], w_ref[...], preferred_element_type=F32)


def _mod_matmul(x, sc, sh, w):
    s, d = x.shape
    n = w.shape[1]
    tm = min(1024, s)
    tn = 1024
    return pl.pallas_call(
        _mod_matmul_kernel,
        grid=(s // tm, n // tn),
        in_specs=[pl.BlockSpec((tm, d), lambda i, j: (i, 0)),
                  pl.BlockSpec((1, d), lambda i, j: (0, 0)),
                  pl.BlockSpec((1, d), lambda i, j: (0, 0)),
                  pl.BlockSpec((d, tn), lambda i, j: (0, j))],
        out_specs=pl.BlockSpec((tm, tn), lambda i, j: (i, j)),
        out_shape=jax.ShapeDtypeStruct((s, n), F32),
        scratch_shapes=[pltpu.VMEM((tm, d), BF16)],
        compiler_params=_params(("parallel", "arbitrary")),
        name="in_proj",
    )(x, sc, sh, w)


CONV_ROWS = 32
CONV_COLS = 256


def _fill_halo(ext_ref, prev_ref, x_ref, next_ref, halo):
    i = pl.program_id(0)
    t = x_ref.shape[0]
    ext_ref[halo:halo + t, :] = x_ref[...]
    ext_ref[0:halo, :] = jnp.where(i > 0, prev_ref[...], 0.0)
    ext_ref[halo + t:halo + t + halo, :] = jnp.where(i < pl.num_programs(0) - 1, next_ref[...], 0.0)


def _halo_specs(t, halo, s, width, col):
    hb = t // halo
    last = s // halo - 1
    return [pl.BlockSpec((halo, width), lambda i: (jnp.maximum(i * hb - 1, 0), col)),
            pl.BlockSpec((t, width), lambda i: (i, col)),
            pl.BlockSpec((halo, width), lambda i: (jnp.minimum((i + 1) * hb, last), col))]


SSD_HALO = SUBLANE
SSD_BLOCK = 4 * SSD_CHUNK


def _ssd_chunk(xs, bc_ref, r0, dt_raw, dtb_ref, alog_ref, exp_ref, state_ref, reverse):
    L = SSD_CHUNK
    gw = SSD_WIDTH // SSD_GROUPS
    hpg = SSD_HEADS // SSD_GROUPS
    rows = slice(r0, r0 + L)

    row = lax.broadcasted_iota(jnp.int32, (L, L), 0)
    col = lax.broadcasted_iota(jnp.int32, (L, L), 1)
    mask = (row <= col) if reverse else (row >= col)
    tri = mask.astype(BF16)
    edge = 0 if reverse else L - 1

    dt = _softplus(dt_raw + dtb_ref[...])
    a = -jnp.exp(alog_ref[...])
    da_terms = _split_bf16(dt * a, 3)
    cum = sum(jnp.dot(tri, t, preferred_element_type=F32) for t in da_terms)
    cum_t = sum(lax.dot_general(t, tri, (((0,), (1,)), ((), ())), preferred_element_type=F32)
                for t in da_terms)
    total = cum[edge:edge + 1, :]
    ecum = jnp.exp(cum)
    stacked = jnp.concatenate([dt, ecum, dt * jnp.exp(total - cum)], axis=0)
    expand = exp_ref[...]
    wide = sum(jnp.dot(t, expand, preferred_element_type=F32) for t in _split_bf16(stacked, 2))
    dt_x, ecum_x, dts_x = wide[0:L], wide[L:2 * L], wide[2 * L:3 * L]
    cdecay_x = ecum_x[edge:edge + 1, :]

    xdt_b = (xs * dt_x).astype(BF16)
    xdt_s = (xs * dts_x).astype(BF16)

    halves = []
    for g in range(SSD_GROUPS):
        bg = bc_ref[rows, g * SSD_STATE:(g + 1) * SSD_STATE].astype(BF16)
        cg = bc_ref[rows, (SSD_GROUPS + g) * SSD_STATE:(SSD_GROUPS + g + 1) * SSD_STATE].astype(BF16)
        cb = lax.dot_general(cg, bg, (((1,), (1,)), ((), ())), preferred_element_type=F32)
        prev = state_ref[g]
        y_off = jnp.dot(cg, prev.astype(BF16), preferred_element_type=F32)
        y_off = y_off * ecum_x[:, g * gw:(g + 1) * gw]
        heads = []
        for r in range(hpg):
            h = g * hpg + r
            seg = cum[:, h:h + 1] - cum_t[h:h + 1, :]
            m = (cb * jnp.exp(jnp.where(mask, seg, -jnp.inf))).astype(BF16)
            heads.append(jnp.dot(m, xdt_b[:, h * SSD_HEAD_DIM:(h + 1) * SSD_HEAD_DIM],
                                 preferred_element_type=F32))
        halves.append(jnp.concatenate(heads, axis=1) + y_off)
        new = lax.dot_general(bg, xdt_s[:, g * gw:(g + 1) * gw], (((0,), (0,)), ((), ())),
                              preferred_element_type=F32)
        state_ref[g] = prev * cdecay_x[:, g * gw:(g + 1) * gw] + new
    return halves


def _ssd_fwd_kernel(prev_ref, cur_ref, next_ref, dt_ref, cw_ref, cbias_ref, dtb_ref, alog_ref, exp_ref,
                    y_ref, xbc_ref, ext_ref, state_ref):
    @pl.when(pl.program_id(0) == 0)
    def _():
        state_ref[...] = jnp.zeros_like(state_ref)

    _fill_halo(ext_ref, prev_ref, cur_ref, next_ref, SSD_HALO)
    t = cur_ref.shape[0]
    pad = (SSD_CONV_WIDTH - 1) // 2
    for r0 in range(0, t, CONV_ROWS):
        for c0 in range(0, XBC_WIDTH, CONV_COLS):
            acc = jnp.broadcast_to(cbias_ref[:, c0:c0 + CONV_COLS], (CONV_ROWS, CONV_COLS))
            for tap in range(SSD_CONV_WIDTH):
                start = SSD_HALO - pad + tap + r0
                acc = acc + cw_ref[tap:tap + 1, c0:c0 + CONV_COLS] * ext_ref[start:start + CONV_ROWS,
                                                                             c0:c0 + CONV_COLS]
            xbc_ref[r0:r0 + CONV_ROWS, c0:c0 + CONV_COLS] = _silu(acc)

    gw = SSD_WIDTH // SSD_GROUPS
    bc_ref = xbc_ref.at[:, SSD_WIDTH:XBC_WIDTH]
    for r0 in range(0, t, SSD_CHUNK):
        halves = _ssd_chunk(xbc_ref[r0:r0 + SSD_CHUNK, 0:SSD_WIDTH], bc_ref, r0,
                            dt_ref[r0:r0 + SSD_CHUNK, 0:SSD_HEADS], dtb_ref, alog_ref, exp_ref, state_ref,
                            reverse=False)
        for g in range(SSD_GROUPS):
            y_ref[r0:r0 + SSD_CHUNK, g * gw:(g + 1) * gw] = halves[g]


def _ssd_fwd(proj, conv_w, conv_b, dt_bias, a_log, expand):
    s = proj.shape[0]
    t = min(SSD_BLOCK, s)
    fix = lambda i: (0, 0)
    return pl.pallas_call(
        _ssd_fwd_kernel,
        grid=(s // t,),
        in_specs=_halo_specs(t, SSD_HALO, s, XBC_WIDTH, 0) + [
            pl.BlockSpec((t, LANE), lambda i: (i, COL_DT // LANE)),
            pl.BlockSpec((SSD_CONV_WIDTH, XBC_WIDTH), fix), pl.BlockSpec((1, XBC_WIDTH), fix),
            pl.BlockSpec((1, SSD_HEADS), fix), pl.BlockSpec((1, SSD_HEADS), fix),
            pl.BlockSpec((SSD_HEADS, SSD_WIDTH), fix)],
        out_specs=[pl.BlockSpec((t, SSD_WIDTH), lambda i: (i, 0)),
                   pl.BlockSpec((t, XBC_WIDTH), lambda i: (i, 0))],
        out_shape=[jax.ShapeDtypeStruct((s, SSD_WIDTH), F32), jax.ShapeDtypeStruct((s, XBC_WIDTH), F32)],
        scratch_shapes=[pltpu.VMEM((t + 2 * SSD_HALO, XBC_WIDTH), F32),
                        pltpu.VMEM((SSD_GROUPS, SSD_STATE, SSD_WIDTH // SSD_GROUPS), F32)],
        compiler_params=_params(("arbitrary",)),
        name="ssd_fwd",
    )(proj, proj, proj, proj, conv_w, conv_b.reshape(1, -1), dt_bias.reshape(1, -1), a_log.reshape(1, -1),
      expand)


def _ssd_bwd_kernel(xs_ref, bc_ref, dt_ref, z_ref, yf_ref, dtb_ref, alog_ref, exp_ref, dskip_ref, nw_ref,
                    o_ref, state_ref):
    @pl.when(pl.program_id(0) == 0)
    def _():
        state_ref[...] = jnp.zeros_like(state_ref)

    t = xs_ref.shape[0]
    for r0 in reversed(range(0, t, SSD_CHUNK)):
        rows = slice(r0, r0 + SSD_CHUNK)
        xs = xs_ref[rows, :]
        halves = _ssd_chunk(xs, bc_ref, r0, dt_ref[rows, SSD_HEADS:2 * SSD_HEADS], dtb_ref, alog_ref, exp_ref,
                            state_ref, reverse=True)
        y = jnp.concatenate(halves, axis=1) + yf_ref[rows, :] + dskip_ref[...] * xs
        v = y * _silu(z_ref[rows, :])
        ms = jnp.mean(v * v, axis=-1, keepdims=True)
        o_ref[rows, :] = (v * lax.rsqrt(ms + EPS) * nw_ref[...]).astype(o_ref.dtype)


def _ssd_bwd(xbc, proj, y_fwd, dt_bias, a_log, expand, d_x, norm_w):
    s = xbc.shape[0]
    t = min(SSD_BLOCK, s)
    nb = s // t
    rev = lambda col: (lambda i: (nb - 1 - i, col))
    fix = lambda i: (0, 0)
    vec = pl.BlockSpec((1, SSD_WIDTH), fix)
    return pl.pallas_call(
        _ssd_bwd_kernel,
        grid=(nb,),
        in_specs=[pl.BlockSpec((t, SSD_WIDTH), rev(0)),
                  pl.BlockSpec((t, BC_WIDTH), rev(SSD_WIDTH // BC_WIDTH)),
                  pl.BlockSpec((t, LANE), rev(COL_DT // LANE)),
                  pl.BlockSpec((t, SSD_WIDTH), rev(COL_Z // SSD_WIDTH)),
                  pl.BlockSpec((t, SSD_WIDTH), rev(0)),
                  pl.BlockSpec((1, SSD_HEADS), fix), pl.BlockSpec((1, SSD_HEADS), fix),
                  pl.BlockSpec((SSD_HEADS, SSD_WIDTH), fix), vec, vec],
        out_specs=pl.BlockSpec((t, SSD_WIDTH), rev(0)),
        out_shape=jax.ShapeDtypeStruct((s, SSD_WIDTH), BF16),
        scratch_shapes=[pltpu.VMEM((SSD_GROUPS, SSD_STATE, SSD_WIDTH // SSD_GROUPS), F32)],
        compiler_params=_params(("arbitrary",)),
        name="ssd_bwd",
    )(xbc, xbc, proj, proj, y_fwd, dt_bias.reshape(1, -1), a_log.reshape(1, -1), expand, d_x,
      norm_w.reshape(1, -1))


CONF_HALO = 16


def _conf_kernel(ap_ref, a_ref, an_ref, bp_ref, b_ref, bn_ref, w_ref, cb_ref, g_ref, beta_ref,
                 o_ref, ext_ref, gate_ref, shift_ref, conv_ref):
    _fill_halo(ext_ref, ap_ref, a_ref, an_ref, CONF_HALO)
    _fill_halo(gate_ref, bp_ref, b_ref, bn_ref, CONF_HALO)
    ext_ref[...] = ext_ref[...] * jax.nn.sigmoid(gate_ref[...])
    t, c = a_ref.shape
    span = shift_ref.shape[1]
    for ph in range(1, SUBLANE):
        shift_ref[ph - 1] = ext_ref[ph:ph + span, :]
    pad = (CONF_CONV_WIDTH - 1) // 2
    for r0 in range(0, t, CONV_ROWS):
        for c0 in range(0, c, CONV_COLS):
            acc = jnp.broadcast_to(cb_ref[:, c0:c0 + CONV_COLS], (CONV_ROWS, CONV_COLS))
            for tap in range(CONF_CONV_WIDTH):
                q, ph = divmod(CONF_HALO - pad + tap, SUBLANE)
                lo = q * SUBLANE + r0
                if ph == 0:
                    win = ext_ref[lo:lo + CONV_ROWS, c0:c0 + CONV_COLS]
                else:
                    win = shift_ref[ph - 1, lo:lo + CONV_ROWS, c0:c0 + CONV_COLS]
                acc = acc + w_ref[tap:tap + 1, c0:c0 + CONV_COLS] * win
            conv_ref[r0:r0 + CONV_ROWS, c0:c0 + CONV_COLS] = acc
    o_ref[...] = _silu(_layer_norm(conv_ref[...], g_ref[...], beta_ref[...])).astype(o_ref.dtype)


def _conformer(proj, conv_w, conv_b, ln_g, ln_b):
    s = proj.shape[0]
    t = min(512, s)
    ca = COL_GLU // CONF_CH
    fix = lambda i: (0, 0)
    vec = pl.BlockSpec((1, CONF_CH), fix)
    span = t + 2 * CONF_HALO - SUBLANE
    return pl.pallas_call(
        _conf_kernel,
        grid=(s // t,),
        in_specs=_halo_specs(t, CONF_HALO, s, CONF_CH, ca) + _halo_specs(t, CONF_HALO, s, CONF_CH, ca + 1) + [
            pl.BlockSpec((CONF_CONV_WIDTH, CONF_CH), fix), vec, vec, vec],
        out_specs=pl.BlockSpec((t, CONF_CH), lambda i: (i, 0)),
        out_shape=jax.ShapeDtypeStruct((s, CONF_CH), BF16),
        scratch_shapes=[pltpu.VMEM((t + 2 * CONF_HALO, CONF_CH), F32),
                        pltpu.VMEM((t + 2 * CONF_HALO, CONF_CH), F32),
                        pltpu.VMEM((SUBLANE - 1, span, CONF_CH), F32),
                        pltpu.VMEM((t, CONF_CH), F32)],
        compiler_params=_params(("parallel",)),
        name="conformer",
    )(proj, proj, proj, proj, proj, proj, conv_w, conv_b.reshape(1, -1), ln_g.reshape(1, -1),
      ln_b.reshape(1, -1))


def _out_proj_kernel(y_ref, u_ref, wy_ref, wu_ref, x_ref, gate_ref, g_ref, b_ref, sc_ref, sh_ref,
                     xo_ref, h_ref, *, alpha):
    mix = jnp.dot(y_ref[...], wy_ref[...], preferred_element_type=F32)
    mix = mix + jnp.dot(u_ref[...], wu_ref[...], preferred_element_type=F32)
    xn = _layer_norm(alpha * x_ref[...] + (1.0 + gate_ref[...]) * mix, g_ref[...], b_ref[...])
    xo_ref[...] = xn
    h_ref[...] = (xn * (1.0 + sc_ref[...]) + sh_ref[...]).astype(h_ref.dtype)


def _out_proj(y_ssd, u, w_out, x, gate, ln_g, ln_b, sc, sh, alpha):
    s, d = x.shape
    tm = min(512, s)
    half = w_out.shape[0] // 2
    row = lambda i: (i, 0)
    fix = lambda i: (0, 0)
    vec = pl.BlockSpec((1, d), fix)
    return pl.pallas_call(
        functools.partial(_out_proj_kernel, alpha=alpha),
        grid=(s // tm,),
        in_specs=[pl.BlockSpec((tm, half), row), pl.BlockSpec((tm, half), row),
                  pl.BlockSpec((half, d), lambda i: (0, 0)), pl.BlockSpec((half, d), lambda i: (1, 0)),
                  pl.BlockSpec((tm, d), row), vec, vec, vec, vec, vec],
        out_specs=[pl.BlockSpec((tm, d), row), pl.BlockSpec((tm, d), row)],
        out_shape=[jax.ShapeDtypeStruct((s, d), F32), jax.ShapeDtypeStruct((s, d), BF16)],
        compiler_params=_params(("parallel",)),
        name="out_proj",
    )(y_ssd, u, w_out, w_out, x, gate, ln_g.reshape(1, -1), ln_b.reshape(1, -1), sc, sh)


def _ffn_kernel(h_ref, wg_ref, wu_ref, wd_ref, x_ref, gate_ref, g_ref, b_ref, o_ref, acc_ref, *, alpha):
    j = pl.program_id(1)

    @pl.when(j == 0)
    def _():
        acc_ref[...] = jnp.zeros_like(acc_ref)

    h = h_ref[...]
    a = jnp.dot(h, wg_ref[...], preferred_element_type=F32)
    b = jnp.dot(h, wu_ref[...], preferred_element_type=F32)
    acc_ref[...] += jnp.dot((_silu(a) * b).astype(BF16), wd_ref[...], preferred_element_type=F32)

    @pl.when(j == pl.num_programs(1) - 1)
    def _():
        o_ref[...] = _layer_norm(alpha * x_ref[...] + (1.0 + gate_ref[...]) * acc_ref[...],
                                 g_ref[...], b_ref[...])


def _dense_ffn(h, wg, wu, wd, x, gate, ln_g, ln_b, alpha):
    s, d = x.shape
    f = wg.shape[1]
    tm = min(512, s)
    tf = 512
    row = lambda i, j: (i, 0)
    vec = pl.BlockSpec((1, d), lambda i, j: (0, 0))
    return pl.pallas_call(
        functools.partial(_ffn_kernel, alpha=alpha),
        grid=(s // tm, f // tf),
        in_specs=[pl.BlockSpec((tm, d), row),
                  pl.BlockSpec((d, tf), lambda i, j: (0, j)), pl.BlockSpec((d, tf), lambda i, j: (0, j)),
                  pl.BlockSpec((tf, d), lambda i, j: (j, 0)),
                  pl.BlockSpec((tm, d), row), vec, vec, vec],
        out_specs=pl.BlockSpec((tm, d), row),
        out_shape=jax.ShapeDtypeStruct((s, d), F32),
        scratch_shapes=[pltpu.VMEM((tm, d), F32)],
        compiler_params=_params(("parallel", "arbitrary")),
        name="dense_ffn",
    )(h, wg, wu, wd, x, gate, ln_g.reshape(1, -1), ln_b.reshape(1, -1))


def _router_kernel(x_ref, sc_ref, sh_ref, wr_ref, route_ref, cnt_ref, carry_ref):
    i = pl.program_id(0)
    tb = x_ref.shape[0]
    e = N_EXPERTS

    @pl.when(i == 0)
    def _():
        carry_ref[...] = jnp.zeros_like(carry_ref)

    h = x_ref[...] * (1.0 + sc_ref[...]) + sh_ref[...]
    logits = lax.dot_general(wr_ref[...], h, (((1,), (1,)), ((), ())), precision=lax.Precision.HIGHEST,
                             preferred_element_type=F32)
    eid = lax.broadcasted_iota(jnp.int32, (e, tb), 0).astype(F32)
    m1 = jnp.max(logits, axis=0, keepdims=True)
    i1 = jnp.min(jnp.where(logits == m1, eid, float(e)), axis=0, keepdims=True)
    rest = jnp.where(eid == i1, -jnp.inf, logits)
    m2 = jnp.max(rest, axis=0, keepdims=True)
    i2 = jnp.min(jnp.where(rest == m2, eid, float(e)), axis=0, keepdims=True)
    p2 = jnp.exp(m2 - m1)
    g1 = 1.0 / (1.0 + p2)
    g2 = p2 / (1.0 + p2)
    sel1 = eid == i1
    sel2 = eid == i2
    onehot = jnp.where(sel1 | sel2, 1.0, 0.0)
    r_i = lax.broadcasted_iota(jnp.int32, (tb, tb), 0)
    c_i = lax.broadcasted_iota(jnp.int32, (tb, tb), 1)
    before = (r_i < c_i).astype(BF16)
    rank = jnp.dot(onehot.astype(BF16), before, preferred_element_type=F32) + carry_ref[:, 0:1]
    r1 = jnp.sum(jnp.where(sel1, rank, 0.0), axis=0, keepdims=True)
    r2 = jnp.sum(jnp.where(sel2, rank, 0.0), axis=0, keepdims=True)
    zero = jnp.zeros_like(g1)
    route_ref[...] = jnp.concatenate(
        [i1, i2, r1, r2, g1, g2, zero, zero], axis=0)
    carry_ref[...] = carry_ref[...] + jnp.sum(onehot, axis=1, keepdims=True)
    cnt_ref[...] = carry_ref[...]


def _router(x, sc, sh, w_router_t):
    s, d = x.shape
    tb = min(512, s)
    fix = lambda i: (0, 0)
    return pl.pallas_call(
        _router_kernel,
        grid=(s // tb,),
        in_specs=[pl.BlockSpec((tb, d), lambda i: (i, 0)), pl.BlockSpec((1, d), fix),
                  pl.BlockSpec((1, d), fix), pl.BlockSpec((N_EXPERTS, d), fix)],
        out_specs=[pl.BlockSpec((8, tb), lambda i: (0, i)), pl.BlockSpec((N_EXPERTS, LANE), fix)],
        out_shape=[jax.ShapeDtypeStruct((8, s), F32), jax.ShapeDtypeStruct((N_EXPERTS, LANE), F32)],
        scratch_shapes=[pltpu.VMEM((N_EXPERTS, LANE), F32)],
        compiler_params=_params(("arbitrary",)),
        name="moe_router",
    )(x, sc, sh, w_router_t)


def _moe_kernel(be_ref, nsub_ref, xb_ref, wg_ref, wu_ref, wd_ref, o_ref, acc_ref):
    b = pl.program_id(0)
    j = pl.program_id(1)
    last = pl.num_programs(1) - 1
    nsub = nsub_ref[b]

    @pl.when(j == 0)
    def _():
        acc_ref[...] = jnp.zeros_like(acc_ref)

    @pl.when(nsub > 0)
    def _():
        wg = wg_ref[...].astype(BF16)
        wu = wu_ref[...].astype(BF16)
        wd = wd_ref[...].astype(BF16)
        for k in range(xb_ref.shape[0] // MOE_SUB):
            rows = slice(k * MOE_SUB, (k + 1) * MOE_SUB)

            @pl.when(nsub > k)
            def _():
                xb = xb_ref[rows, :]
                a = jnp.dot(xb, wg, preferred_element_type=F32)
                u = jnp.dot(xb, wu, preferred_element_type=F32)
                acc_ref[rows, :] += jnp.dot((_silu(a) * u).astype(BF16), wd, preferred_element_type=F32)

    @pl.when(j == last)
    def _():
        o_ref[...] = acc_ref[...].astype(o_ref.dtype)


MOE_BLOCK = 1024
MOE_SUB = 512
MOE_FF_TILE = 256


def _moe_experts(xb, block_expert, block_nsub, wg, wu, wd, tb):
    p, d = xb.shape
    f = wg.shape[2]
    tf = MOE_FF_TILE
    nf = f // tf
    jj = lambda b, j, nsub: jnp.where(nsub[b] > 0, j, nf - 1)
    return pl.pallas_call(
        _moe_kernel,
        grid_spec=pltpu.PrefetchScalarGridSpec(
            num_scalar_prefetch=2,
            grid=(p // tb, nf),
            in_specs=[pl.BlockSpec((tb, d), lambda b, j, be, valid: (b, 0)),
                      pl.BlockSpec((None, d, tf), lambda b, j, be, valid: (be[b], 0, jj(b, j, valid))),
                      pl.BlockSpec((None, d, tf), lambda b, j, be, valid: (be[b], 0, jj(b, j, valid))),
                      pl.BlockSpec((None, tf, d), lambda b, j, be, valid: (be[b], jj(b, j, valid), 0))],
            out_specs=pl.BlockSpec((tb, d), lambda b, j, be, valid: (b, 0)),
            scratch_shapes=[pltpu.VMEM((tb, d), F32)]),
        out_shape=jax.ShapeDtypeStruct((p, d), BF16),
        compiler_params=_params(("arbitrary", "arbitrary")),
        name="moe_experts",
    )(block_expert, block_nsub, xb, wg, wu, wd)


def _res_ln_kernel(x_ref, y1_ref, y2_ref, g1_ref, g2_ref, gate_ref, g_ref, b_ref, o_ref, *, alpha):
    f = g1_ref[...] * y1_ref[...].astype(F32) + g2_ref[...] * y2_ref[...].astype(F32)
    o_ref[...] = _layer_norm(alpha * x_ref[...] + (1.0 + gate_ref[...]) * f, g_ref[...], b_ref[...])


def _res_ln(x, y1, y2, g1, g2, gate, ln_g, ln_b, alpha):
    s, d = x.shape
    t = min(512, s)
    row = lambda i: (i, 0)
    vec = pl.BlockSpec((1, d), lambda i: (0, 0))
    col = pl.BlockSpec((t, 1), row)
    return pl.pallas_call(
        functools.partial(_res_ln_kernel, alpha=alpha),
        grid=(s // t,),
        in_specs=[pl.BlockSpec((t, d), row), pl.BlockSpec((t, d), row), pl.BlockSpec((t, d), row), col, col,
                  vec, vec, vec],
        out_specs=pl.BlockSpec((t, d), row),
        out_shape=jax.ShapeDtypeStruct((s, d), F32),
        compiler_params=_params(("parallel",)),
        name="res_ln",
    )(x, y1, y2, g1, g2, gate, ln_g.reshape(1, -1), ln_b.reshape(1, -1))


def _moe_ffn(x, h, sc, sh, w_router, wg, wu, wd, gate, ln_g, ln_b, alpha):
    s, d = x.shape
    tb = min(MOE_BLOCK, s)
    route, counts = _router(x, sc, sh, w_router.T)
    e1 = route[0].astype(jnp.int32)
    e2 = route[1].astype(jnp.int32)
    counts = counts[:, 0].astype(jnp.int32)
    padded = (counts + tb - 1) // tb * tb
    pad_end = jnp.cumsum(padded)
    pad_start = pad_end - padded
    d1 = pad_start[e1] + route[2].astype(jnp.int32)
    d2 = pad_start[e2] + route[3].astype(jnp.int32)
    n_blocks = (2 * s) // tb + N_EXPERTS
    p_tot = n_blocks * tb
    tok = jnp.arange(s, dtype=jnp.int32)
    tok_pad = jnp.zeros((p_tot,), jnp.int32).at[d1].set(tok).at[d2].set(tok)
    blk_start = jnp.arange(n_blocks, dtype=jnp.int32) * tb
    block_valid = (blk_start < pad_end[-1]).astype(jnp.int32)
    block_expert = jnp.minimum(jnp.searchsorted(pad_end, blk_start, side="right"), N_EXPERTS - 1)
    last_expert = block_expert[jnp.maximum(pad_end[-1] // tb - 1, 0)]
    block_expert = jnp.where(block_valid == 1, block_expert, last_expert).astype(jnp.int32)
    sub = min(MOE_SUB, tb)
    rows_left = counts[block_expert] - (blk_start - pad_start[block_expert])
    block_nsub = jnp.where(block_valid == 1, jnp.clip((rows_left + sub - 1) // sub, 0, tb // sub), 0)
    xb = jnp.take(h, tok_pad, axis=0)
    yb = _moe_experts(xb, block_expert, block_nsub.astype(jnp.int32), wg, wu, wd, tb)
    return _res_ln(x, jnp.take(yb, d1, axis=0), jnp.take(yb, d2, axis=0), route[4][:, None], route[5][:, None],
                   gate, ln_g, ln_b, alpha)


def kernel(x, c, w_ada, b_ada, w_in, ssd_conv_w, ssd_conv_b, dt_bias, a_log, d_skip, ssd_norm_w, conf_conv_w, conf_conv_b, conf_ln_g, conf_ln_b, w_out, ln1_g, ln1_b, ln2_g, ln2_b, ffn_w_gate, ffn_w_up, ffn_w_down, moe_router, moe_w_gate, moe_w_up, moe_w_down):
    bsz, s, d = x.shape
    assert bsz == 1
    depth = w_ada.shape[0]
    alpha = (2 * depth) ** 0.25
    xs = x.reshape(s, d)

    mod = _ada_mod(c, w_ada, b_ada)
    z_end = SSD_WIDTH
    xbc_end = z_end + XBC_WIDTH
    dt_end = xbc_end + 2 * SSD_HEADS
    w_proj = jnp.concatenate(
        [w_in[:, :, z_end:dt_end].astype(BF16),
         jnp.zeros((depth, d, DT_PAD - 2 * SSD_HEADS), BF16),
         w_in[:, :, :z_end].astype(BF16),
         w_in[:, :, dt_end:].astype(BF16)], axis=2)
    w_out_b = w_out.astype(BF16)
    expand = jnp.repeat(jnp.eye(SSD_HEADS, dtype=BF16), SSD_HEAD_DIM, axis=1)
    d_x = jnp.repeat(d_skip, SSD_HEAD_DIM, axis=1)

    for l in range(depth):
        sh_m, sc_m, g_m, sh_f, sc_f, g_f = [mod[l, :, k * d:(k + 1) * d] for k in range(6)]
        proj = _mod_matmul(xs, sc_m, sh_m, w_proj[l])
        y_fwd, xbc = _ssd_fwd(proj, ssd_conv_w[l], ssd_conv_b[l], dt_bias[l, 0], a_log[l, 0], expand)
        y_ssd = _ssd_bwd(xbc, proj, y_fwd, dt_bias[l, 1], a_log[l, 1], expand, d_x[l:l + 1], ssd_norm_w[l])
        u = _conformer(proj, conf_conv_w[l], conf_conv_b[l], conf_ln_g[l], conf_ln_b[l])
        xs, h = _out_proj(y_ssd, u, w_out_b[l], xs, g_m, ln1_g[l], ln1_b[l], sc_f, sh_f, alpha)
        i = l // 2
        if l % 2 == 0:
            xs = _dense_ffn(h, ffn_w_gate[i].astype(BF16), ffn_w_up[i].astype(BF16),
                            ffn_w_down[i].astype(BF16), xs, g_f, ln2_g[l], ln2_b[l], alpha)
        else:
            xs = _moe_ffn(xs, h, sc_f, sh_f, moe_router[i], moe_w_gate[i], moe_w_up[i], moe_w_down[i], g_f,
                          ln2_g[l], ln2_b[l], alpha)
    return xs.reshape(bsz, s, d)
```

```python
import functools

import jax
import jax.numpy as jnp
from jax import lax
from jax.experimental import pallas as pl
from jax.experimental.pallas import tpu as pltpu

F32 = jnp.float32
BF16 = jnp.bfloat16

SSD_HEADS = 16
SSD_HEAD_DIM = 64
SSD_WIDTH = SSD_HEADS * SSD_HEAD_DIM
SSD_GROUPS = 2
SSD_STATE = 128
SSD_CHUNK = 128
SSD_CONV_WIDTH = 5
CONF_CH = 1024
CONF_CONV_WIDTH = 31
N_EXPERTS = 8
EPS = 1e-5

BC_WIDTH = 2 * SSD_GROUPS * SSD_STATE
XBC_WIDTH = SSD_WIDTH + BC_WIDTH
COL_DT = XBC_WIDTH
DT_PAD = 512
COL_Z = COL_DT + DT_PAD
COL_GLU = COL_Z + SSD_WIDTH
PROJ_COLS = COL_GLU + 2 * CONF_CH

LANE = 128
SUBLANE = 8
VMEM_LIMIT = 56 * 1024 * 1024


def _params(sem, vmem=VMEM_LIMIT):
    return pltpu.CompilerParams(dimension_semantics=sem, vmem_limit_bytes=vmem)


def _silu(v):
    return v * jax.nn.sigmoid(v)


def _softplus(v):
    return jnp.maximum(v, 0.0) + jnp.log1p(jnp.exp(-jnp.abs(v)))


def _layer_norm(v, g, b):
    mu = jnp.mean(v, axis=-1, keepdims=True)
    d = v - mu
    var = jnp.mean(d * d, axis=-1, keepdims=True)
    return d * lax.rsqrt(var + EPS) * g + b


def _split_bf16(v, terms):
    out = []
    for _ in range(terms):
        t = v.astype(BF16)
        out.append(t)
        v = v - t.astype(F32)
    return out


def _ada_kernel(cb_ref, w_ref, b_ref, o_ref):
    cb = cb_ref[...]
    act = _silu(cb)
    tn = w_ref.shape[-1]
    parts = [jnp.sum(w_ref[:, j * LANE:(j + 1) * LANE] * act, axis=0, keepdims=True)
             for j in range(tn // LANE)]
    o_ref[...] = jnp.concatenate(parts, axis=1) + b_ref[...]


def _ada_mod(c, w_ada, b_ada):
    depth, d, n = w_ada.shape
    tn = 1024
    cb = jnp.broadcast_to(c.reshape(d, 1), (d, LANE))
    return pl.pallas_call(
        _ada_kernel,
        grid=(depth, n // tn),
        in_specs=[pl.BlockSpec((d, LANE), lambda l, j: (0, 0)),
                  pl.BlockSpec((None, d, tn), lambda l, j: (l, 0, j)),
                  pl.BlockSpec((None, 1, tn), lambda l, j: (l, 0, j))],
        out_specs=pl.BlockSpec((None, 1, tn), lambda l, j: (l, 0, j)),
        out_shape=jax.ShapeDtypeStruct((depth, 1, n), F32),
        compiler_params=_params(("parallel", "parallel")),
        name="ada_mod",
    )(cb, w_ada, b_ada.reshape(depth, 1, n))


def _mod_matmul_kernel(x_ref, sc_ref, sh_ref, w_ref, o_ref, h_ref):
    @pl.when(pl.program_id(1) == 0)
    def _():
        h_ref[...] = (x_ref[...] * (1.0 + sc_ref[...]) + sh_ref[...]).astype(BF16)

    o_ref[...] = jnp.dot(h_ref[...], w_ref[...], preferred_element_type=F32)


def _mod_matmul(x, sc, sh, w):
    s, d = x.shape
    n = w.shape[1]
    tm = min(1024, s)
    tn = 1024
    return pl.pallas_call(
        _mod_matmul_kernel,
        grid=(s // tm, n // tn),
        in_specs=[pl.BlockSpec((tm, d), lambda i, j: (i, 0)),
                  pl.BlockSpec((1, d), lambda i, j: (0, 0)),
                  pl.BlockSpec((1, d), lambda i, j: (0, 0)),
                  pl.BlockSpec((d, tn), lambda i, j: (0, j))],
        out_specs=pl.BlockSpec((tm, tn), lambda i, j: (i, j)),
        out_shape=jax.ShapeDtypeStruct((s, n), F32),
        scratch_shapes=[pltpu.VMEM((tm, d), BF16)],
        compiler_params=_params(("parallel", "arbitrary")),
        name="in_proj",
    )(x, sc, sh, w)


CONV_ROWS = 32
CONV_COLS = 256


def _fill_halo(ext_ref, prev_ref, x_ref, next_ref, halo):
    i = pl.program_id(0)
    t = x_ref.shape[0]
    ext_ref[halo:halo + t, :] = x_ref[...]
    ext_ref[0:halo, :] = jnp.where(i > 0, prev_ref[...], 0.0)
    ext_ref[halo + t:halo + t + halo, :] = jnp.where(i < pl.num_programs(0) - 1, next_ref[...], 0.0)


def _halo_specs(t, halo, s, width, col):
    hb = t // halo
    last = s // halo - 1
    return [pl.BlockSpec((halo, width), lambda i: (jnp.maximum(i * hb - 1, 0), col)),
            pl.BlockSpec((t, width), lambda i: (i, col)),
            pl.BlockSpec((halo, width), lambda i: (jnp.minimum((i + 1) * hb, last), col))]


SSD_HALO = SUBLANE
SSD_BLOCK = 4 * SSD_CHUNK


def _ssd_chunk(xs, bc_ref, r0, dt_raw, dtb_ref, alog_ref, exp_ref, state_ref, reverse):
    L = SSD_CHUNK
    gw = SSD_WIDTH // SSD_GROUPS
    hpg = SSD_HEADS // SSD_GROUPS
    rows = slice(r0, r0 + L)

    row = lax.broadcasted_iota(jnp.int32, (L, L), 0)
    col = lax.broadcasted_iota(jnp.int32, (L, L), 1)
    mask = (row <= col) if reverse else (row >= col)
    tri = mask.astype(BF16)
    edge = 0 if reverse else L - 1

    dt = _softplus(dt_raw + dtb_ref[...])
    a = -jnp.exp(alog_ref[...])
    da_terms = _split_bf16(dt * a, 3)
    cum = sum(jnp.dot(tri, t, preferred_element_type=F32) for t in da_terms)
    cum_t = sum(lax.dot_general(t, tri, (((0,), (1,)), ((), ())), preferred_element_type=F32)
                for t in da_terms)
    total = cum[edge:edge + 1, :]
    ecum = jnp.exp(cum)
    stacked = jnp.concatenate([dt, ecum, dt * jnp.exp(total - cum)], axis=0)
    expand = exp_ref[...]
    wide = sum(jnp.dot(t, expand, preferred_element_type=F32) for t in _split_bf16(stacked, 2))
    dt_x, ecum_x, dts_x = wide[0:L], wide[L:2 * L], wide[2 * L:3 * L]
    cdecay_x = ecum_x[edge:edge + 1, :]

    xdt_b = (xs * dt_x).astype(BF16)
    xdt_s = (xs * dts_x).astype(BF16)

    halves = []
    for g in range(SSD_GROUPS):
        bg = bc_ref[rows, g * SSD_STATE:(g + 1) * SSD_STATE].astype(BF16)
        cg = bc_ref[rows, (SSD_GROUPS + g) * SSD_STATE:(SSD_GROUPS + g + 1) * SSD_STATE].astype(BF16)
        cb = lax.dot_general(cg, bg, (((1,), (1,)), ((), ())), preferred_element_type=F32)
        prev = state_ref[g]
        y_off = jnp.dot(cg, prev.astype(BF16), preferred_element_type=F32)
        y_off = y_off * ecum_x[:, g * gw:(g + 1) * gw]
        heads = []
        for r in range(hpg):
            h = g * hpg + r
            seg = cum[:, h:h + 1] - cum_t[h:h + 1, :]
            m = (cb * jnp.exp(jnp.where(mask, seg, -jnp.inf))).astype(BF16)
            heads.append(jnp.dot(m, xdt_b[:, h * SSD_HEAD_DIM:(h + 1) * SSD_HEAD_DIM],
                                 preferred_element_type=F32))
        halves.append(jnp.concatenate(heads, axis=1) + y_off)
        new = lax.dot_general(bg, xdt_s[:, g * gw:(g + 1) * gw], (((0,), (0,)), ((), ())),
                              preferred_element_type=F32)
        state_ref[g] = prev * cdecay_x[:, g * gw:(g + 1) * gw] + new
    return halves


def _ssd_fwd_kernel(prev_ref, cur_ref, next_ref, dt_ref, cw_ref, cbias_ref, dtb_ref, alog_ref, exp_ref,
                    y_ref, xbc_ref, ext_ref, state_ref):
    @pl.when(pl.program_id(0) == 0)
    def _():
        state_ref[...] = jnp.zeros_like(state_ref)

    _fill_halo(ext_ref, prev_ref, cur_ref, next_ref, SSD_HALO)
    t = cur_ref.shape[0]
    pad = (SSD_CONV_WIDTH - 1) // 2
    for r0 in range(0, t, CONV_ROWS):
        for c0 in range(0, XBC_WIDTH, CONV_COLS):
            acc = jnp.broadcast_to(cbias_ref[:, c0:c0 + CONV_COLS], (CONV_ROWS, CONV_COLS))
            for tap in range(SSD_CONV_WIDTH):
                start = SSD_HALO - pad + tap + r0
                acc = acc + cw_ref[tap:tap + 1, c0:c0 + CONV_COLS] * ext_ref[start:start + CONV_ROWS,
                                                                             c0:c0 + CONV_COLS]
            xbc_ref[r0:r0 + CONV_ROWS, c0:c0 + CONV_COLS] = _silu(acc)

    gw = SSD_WIDTH // SSD_GROUPS
    bc_ref = xbc_ref.at[:, SSD_WIDTH:XBC_WIDTH]
    for r0 in range(0, t, SSD_CHUNK):
        halves = _ssd_chunk(xbc_ref[r0:r0 + SSD_CHUNK, 0:SSD_WIDTH], bc_ref, r0,
                            dt_ref[r0:r0 + SSD_CHUNK, 0:SSD_HEADS], dtb_ref, alog_ref, exp_ref, state_ref,
                            reverse=False)
        for g in range(SSD_GROUPS):
            y_ref[r0:r0 + SSD_CHUNK, g * gw:(g + 1) * gw] = halves[g]


def _ssd_fwd(proj, conv_w, conv_b, dt_bias, a_log, expand):
    s = proj.shape[0]
    t = min(SSD_BLOCK, s)
    fix = lambda i: (0, 0)
    return pl.pallas_call(
        _ssd_fwd_kernel,
        grid=(s // t,),
        in_specs=_halo_specs(t, SSD_HALO, s, XBC_WIDTH, 0) + [
            pl.BlockSpec((t, LANE), lambda i: (i, COL_DT // LANE)),
            pl.BlockSpec((SSD_CONV_WIDTH, XBC_WIDTH), fix), pl.BlockSpec((1, XBC_WIDTH), fix),
            pl.BlockSpec((1, SSD_HEADS), fix), pl.BlockSpec((1, SSD_HEADS), fix),
            pl.BlockSpec((SSD_HEADS, SSD_WIDTH), fix)],
        out_specs=[pl.BlockSpec((t, SSD_WIDTH), lambda i: (i, 0)),
                   pl.BlockSpec((t, XBC_WIDTH), lambda i: (i, 0))],
        out_shape=[jax.ShapeDtypeStruct((s, SSD_WIDTH), F32), jax.ShapeDtypeStruct((s, XBC_WIDTH), F32)],
        scratch_shapes=[pltpu.VMEM((t + 2 * SSD_HALO, XBC_WIDTH), F32),
                        pltpu.VMEM((SSD_GROUPS, SSD_STATE, SSD_WIDTH // SSD_GROUPS), F32)],
        compiler_params=_params(("arbitrary",)),
        name="ssd_fwd",
    )(proj, proj, proj, proj, conv_w, conv_b.reshape(1, -1), dt_bias.reshape(1, -1), a_log.reshape(1, -1),
      expand)


def _ssd_bwd_kernel(xs_ref, bc_ref, dt_ref, z_ref, yf_ref, dtb_ref, alog_ref, exp_ref, dskip_ref, nw_ref,
                    o_ref, state_ref):
    @pl.when(pl.program_id(0) == 0)
    def _():
        state_ref[...] = jnp.zeros_like(state_ref)

    t = xs_ref.shape[0]
    for r0 in reversed(range(0, t, SSD_CHUNK)):
        rows = slice(r0, r0 + SSD_CHUNK)
        xs = xs_ref[rows, :]
        halves = _ssd_chunk(xs, bc_ref, r0, dt_ref[rows, SSD_HEADS:2 * SSD_HEADS], dtb_ref, alog_ref, exp_ref,
                            state_ref, reverse=True)
        y = jnp.concatenate(halves, axis=1) + yf_ref[rows, :] + dskip_ref[...] * xs
        v = y * _silu(z_ref[rows, :])
        ms = jnp.mean(v * v, axis=-1, keepdims=True)
        o_ref[rows, :] = (v * lax.rsqrt(ms + EPS) * nw_ref[...]).astype(o_ref.dtype)


def _ssd_bwd(xbc, proj, y_fwd, dt_bias, a_log, expand, d_x, norm_w):
    s = xbc.shape[0]
    t = min(SSD_BLOCK, s)
    nb = s // t
    rev = lambda col: (lambda i: (nb - 1 - i, col))
    fix = lambda i: (0, 0)
    vec = pl.BlockSpec((1, SSD_WIDTH), fix)
    return pl.pallas_call(
        _ssd_bwd_kernel,
        grid=(nb,),
        in_specs=[pl.BlockSpec((t, SSD_WIDTH), rev(0)),
                  pl.BlockSpec((t, BC_WIDTH), rev(SSD_WIDTH // BC_WIDTH)),
                  pl.BlockSpec((t, LANE), rev(COL_DT // LANE)),
                  pl.BlockSpec((t, SSD_WIDTH), rev(COL_Z // SSD_WIDTH)),
                  pl.BlockSpec((t, SSD_WIDTH), rev(0)),
                  pl.BlockSpec((1, SSD_HEADS), fix), pl.BlockSpec((1, SSD_HEADS), fix),
                  pl.BlockSpec((SSD_HEADS, SSD_WIDTH), fix), vec, vec],
        out_specs=pl.BlockSpec((t, SSD_WIDTH), rev(0)),
        out_shape=jax.ShapeDtypeStruct((s, SSD_WIDTH), BF16),
        scratch_shapes=[pltpu.VMEM((SSD_GROUPS, SSD_STATE, SSD_WIDTH // SSD_GROUPS), F32)],
        compiler_params=_params(("arbitrary",)),
        name="ssd_bwd",
    )(xbc, xbc, proj, proj, y_fwd, dt_bias.reshape(1, -1), a_log.reshape(1, -1), expand, d_x,
      norm_w.reshape(1, -1))


CONF_HALO = 16


def _conf_kernel(ap_ref, a_ref, an_ref, bp_ref, b_ref, bn_ref, w_ref, cb_ref, g_ref, beta_ref,
                 o_ref, ext_ref, gate_ref, shift_ref, conv_ref):
    _fill_halo(ext_ref, ap_ref, a_ref, an_ref, CONF_HALO)
    _fill_halo(gate_ref, bp_ref, b_ref, bn_ref, CONF_HALO)
    ext_ref[...] = ext_ref[...] * jax.nn.sigmoid(gate_ref[...])
    t, c = a_ref.shape
    span = shift_ref.shape[1]
    for ph in range(1, SUBLANE):
        shift_ref[ph - 1] = ext_ref[ph:ph + span, :]
    pad = (CONF_CONV_WIDTH - 1) // 2
    for r0 in range(0, t, CONV_ROWS):
        for c0 in range(0, c, CONV_COLS):
            acc = jnp.broadcast_to(cb_ref[:, c0:c0 + CONV_COLS], (CONV_ROWS, CONV_COLS))
            for tap in range(CONF_CONV_WIDTH):
                q, ph = divmod(CONF_HALO - pad + tap, SUBLANE)
                lo = q * SUBLANE + r0
                if ph == 0:
                    win = ext_ref[lo:lo + CONV_ROWS, c0:c0 + CONV_COLS]
                else:
                    win = shift_ref[ph - 1, lo:lo + CONV_ROWS, c0:c0 + CONV_COLS]
                acc = acc + w_ref[tap:tap + 1, c0:c0 + CONV_COLS] * win
            conv_ref[r0:r0 + CONV_ROWS, c0:c0 + CONV_COLS] = acc
    o_ref[...] = _silu(_layer_norm(conv_ref[...], g_ref[...], beta_ref[...])).astype(o_ref.dtype)


def _conformer(proj, conv_w, conv_b, ln_g, ln_b):
    s = proj.shape[0]
    t = min(512, s)
    ca = COL_GLU // CONF_CH
    fix = lambda i: (0, 0)
    vec = pl.BlockSpec((1, CONF_CH), fix)
    span = t + 2 * CONF_HALO - SUBLANE
    return pl.pallas_call(
        _conf_kernel,
        grid=(s // t,),
        in_specs=_halo_specs(t, CONF_HALO, s, CONF_CH, ca) + _halo_specs(t, CONF_HALO, s, CONF_CH, ca + 1) + [
            pl.BlockSpec((CONF_CONV_WIDTH, CONF_CH), fix), vec, vec, vec],
        out_specs=pl.BlockSpec((t, CONF_CH), lambda i: (i, 0)),
        out_shape=jax.ShapeDtypeStruct((s, CONF_CH), BF16),
        scratch_shapes=[pltpu.VMEM((t + 2 * CONF_HALO, CONF_CH), F32),
                        pltpu.VMEM((t + 2 * CONF_HALO, CONF_CH), F32),
                        pltpu.VMEM((SUBLANE - 1, span, CONF_CH), F32),
                        pltpu.VMEM((t, CONF_CH), F32)],
        compiler_params=_params(("parallel",)),
        name="conformer",
    )(proj, proj, proj, proj, proj, proj, conv_w, conv_b.reshape(1, -1), ln_g.reshape(1, -1),
      ln_b.reshape(1, -1))


def _out_proj_kernel(y_ref, u_ref, wy_ref, wu_ref, x_ref, gate_ref, g_ref, b_ref, sc_ref, sh_ref,
                     xo_ref, h_ref, *, alpha):
    half = x_ref.shape[0] // 2
    for r0 in (0, half):
        rows = slice(r0, r0 + half)
        mix = jnp.dot(y_ref[rows, :], wy_ref[...], preferred_element_type=F32)
        mix = mix + jnp.dot(u_ref[rows, :], wu_ref[...], preferred_element_type=F32)
        xn = _layer_norm(alpha * x_ref[rows, :] + (1.0 + gate_ref[...]) * mix, g_ref[...], b_ref[...])
        xo_ref[rows, :] = xn
        h_ref[rows, :] = (xn * (1.0 + sc_ref[...]) + sh_ref[...]).astype(h_ref.dtype)


def _out_proj(y_ssd, u, w_out, x, gate, ln_g, ln_b, sc, sh, alpha):
    s, d = x.shape
    tm = min(512, s)
    half = w_out.shape[0] // 2
    row = lambda i: (i, 0)
    fix = lambda i: (0, 0)
    vec = pl.BlockSpec((1, d), fix)
    return pl.pallas_call(
        functools.partial(_out_proj_kernel, alpha=alpha),
        grid=(s // tm,),
        in_specs=[pl.BlockSpec((tm, half), row), pl.BlockSpec((tm, half), row),
                  pl.BlockSpec((half, d), lambda i: (0, 0)), pl.BlockSpec((half, d), lambda i: (1, 0)),
                  pl.BlockSpec((tm, d), row), vec, vec, vec, vec, vec],
        out_specs=[pl.BlockSpec((tm, d), row), pl.BlockSpec((tm, d), row)],
        out_shape=[jax.ShapeDtypeStruct((s, d), F32), jax.ShapeDtypeStruct((s, d), BF16)],
        compiler_params=_params(("parallel",)),
        name="out_proj",
    )(y_ssd, u, w_out, w_out, x, gate, ln_g.reshape(1, -1), ln_b.reshape(1, -1), sc, sh)


def _ffn_kernel(h_ref, wg_ref, wu_ref, wd_ref, x_ref, gate_ref, g_ref, b_ref, o_ref, acc_ref, *, alpha):
    j = pl.program_id(1)

    @pl.when(j == 0)
    def _():
        acc_ref[...] = jnp.zeros_like(acc_ref)

    h = h_ref[...]
    a = jnp.dot(h, wg_ref[...], preferred_element_type=F32)
    b = jnp.dot(h, wu_ref[...], preferred_element_type=F32)
    acc_ref[...] += jnp.dot((_silu(a) * b).astype(BF16), wd_ref[...], preferred_element_type=F32)

    @pl.when(j == pl.num_programs(1) - 1)
    def _():
        o_ref[...] = _layer_norm(alpha * x_ref[...] + (1.0 + gate_ref[...]) * acc_ref[...],
                                 g_ref[...], b_ref[...])


def _dense_ffn(h, wg, wu, wd, x, gate, ln_g, ln_b, alpha):
    s, d = x.shape
    f = wg.shape[1]
    tm = min(512, s)
    tf = 512
    row = lambda i, j: (i, 0)
    vec = pl.BlockSpec((1, d), lambda i, j: (0, 0))
    return pl.pallas_call(
        functools.partial(_ffn_kernel, alpha=alpha),
        grid=(s // tm, f // tf),
        in_specs=[pl.BlockSpec((tm, d), row),
                  pl.BlockSpec((d, tf), lambda i, j: (0, j)), pl.BlockSpec((d, tf), lambda i, j: (0, j)),
                  pl.BlockSpec((tf, d), lambda i, j: (j, 0)),
                  pl.BlockSpec((tm, d), row), vec, vec, vec],
        out_specs=pl.BlockSpec((tm, d), row),
        out_shape=jax.ShapeDtypeStruct((s, d), F32),
        scratch_shapes=[pltpu.VMEM((tm, d), F32)],
        compiler_params=_params(("parallel", "arbitrary")),
        name="dense_ffn",
    )(h, wg, wu, wd, x, gate, ln_g.reshape(1, -1), ln_b.reshape(1, -1))


def _router_kernel(x_ref, sc_ref, sh_ref, wr_ref, route_ref, cnt_ref, carry_ref):
    i = pl.program_id(0)
    tb = x_ref.shape[0]
    e = N_EXPERTS

    @pl.when(i == 0)
    def _():
        carry_ref[...] = jnp.zeros_like(carry_ref)

    h = x_ref[...] * (1.0 + sc_ref[...]) + sh_ref[...]
    logits = lax.dot_general(wr_ref[...], h, (((1,), (1,)), ((), ())), precision=lax.Precision.HIGHEST,
                             preferred_element_type=F32)
    eid = lax.broadcasted_iota(jnp.int32, (e, tb), 0).astype(F32)
    m1 = jnp.max(logits, axis=0, keepdims=True)
    i1 = jnp.min(jnp.where(logits == m1, eid, float(e)), axis=0, keepdims=True)
    rest = jnp.where(eid == i1, -jnp.inf, logits)
    m2 = jnp.max(rest, axis=0, keepdims=True)
    i2 = jnp.min(jnp.where(rest == m2, eid, float(e)), axis=0, keepdims=True)
    p2 = jnp.exp(m2 - m1)
    g1 = 1.0 / (1.0 + p2)
    g2 = p2 / (1.0 + p2)
    sel1 = eid == i1
    sel2 = eid == i2
    onehot = jnp.where(sel1 | sel2, 1.0, 0.0)
    r_i = lax.broadcasted_iota(jnp.int32, (tb, tb), 0)
    c_i = lax.broadcasted_iota(jnp.int32, (tb, tb), 1)
    before = (r_i < c_i).astype(BF16)
    rank = jnp.dot(onehot.astype(BF16), before, preferred_element_type=F32) + carry_ref[:, 0:1]
    r1 = jnp.sum(jnp.where(sel1, rank, 0.0), axis=0, keepdims=True)
    r2 = jnp.sum(jnp.where(sel2, rank, 0.0), axis=0, keepdims=True)
    zero = jnp.zeros_like(g1)
    route_ref[...] = jnp.concatenate(
        [i1, i2, r1, r2, g1, g2, zero, zero], axis=0)
    carry_ref[...] = carry_ref[...] + jnp.sum(onehot, axis=1, keepdims=True)
    cnt_ref[...] = carry_ref[...]


def _router(x, sc, sh, w_router_t):
    s, d = x.shape
    tb = min(512, s)
    fix = lambda i: (0, 0)
    return pl.pallas_call(
        _router_kernel,
        grid=(s // tb,),
        in_specs=[pl.BlockSpec((tb, d), lambda i: (i, 0)), pl.BlockSpec((1, d), fix),
                  pl.BlockSpec((1, d), fix), pl.BlockSpec((N_EXPERTS, d), fix)],
        out_specs=[pl.BlockSpec((8, tb), lambda i: (0, i)), pl.BlockSpec((N_EXPERTS, LANE), fix)],
        out_shape=[jax.ShapeDtypeStruct((8, s), F32), jax.ShapeDtypeStruct((N_EXPERTS, LANE), F32)],
        scratch_shapes=[pltpu.VMEM((N_EXPERTS, LANE), F32)],
        compiler_params=_params(("arbitrary",)),
        name="moe_router",
    )(x, sc, sh, w_router_t)


def _moe_kernel(be_ref, nsub_ref, xb_ref, wg_ref, wu_ref, wd_ref, o_ref, acc_ref):
    b = pl.program_id(0)
    j = pl.program_id(1)
    last = pl.num_programs(1) - 1
    nsub = nsub_ref[b]

    @pl.when(j == 0)
    def _():
        acc_ref[...] = jnp.zeros_like(acc_ref)

    @pl.when(nsub > 0)
    def _():
        wg = wg_ref[...].astype(BF16)
        wu = wu_ref[...].astype(BF16)
        wd = wd_ref[...].astype(BF16)
        for k in range(xb_ref.shape[0] // MOE_SUB):
            rows = slice(k * MOE_SUB, (k + 1) * MOE_SUB)

            @pl.when(nsub > k)
            def _():
                xb = xb_ref[rows, :]
                a = jnp.dot(xb, wg, preferred_element_type=F32)
                u = jnp.dot(xb, wu, preferred_element_type=F32)
                acc_ref[rows, :] += jnp.dot((_silu(a) * u).astype(BF16), wd, preferred_element_type=F32)

    @pl.when(j == last)
    def _():
        o_ref[...] = acc_ref[...].astype(o_ref.dtype)


MOE_BLOCK = 1024
MOE_SUB = 512
MOE_FF_TILE = 512
MOE_VMEM_LIMIT = 60 * 1024 * 1024


def _moe_experts(xb, block_expert, block_nsub, wg, wu, wd, layer, tb):
    p, d = xb.shape
    f = wg.shape[3]
    tf = MOE_FF_TILE
    nf = f // tf
    jj = lambda b, j, nsub: jnp.where(nsub[b] > 0, j, nf - 1)
    once = pl.Buffered(1)
    return pl.pallas_call(
        _moe_kernel,
        grid_spec=pltpu.PrefetchScalarGridSpec(
            num_scalar_prefetch=2,
            grid=(p // tb, nf),
            in_specs=[pl.BlockSpec((tb, d), lambda b, j, be, ns: (b, 0), pipeline_mode=once),
                      pl.BlockSpec((None, None, d, tf), lambda b, j, be, ns: (layer, be[b], 0, jj(b, j, ns))),
                      pl.BlockSpec((None, None, d, tf), lambda b, j, be, ns: (layer, be[b], 0, jj(b, j, ns))),
                      pl.BlockSpec((None, None, tf, d), lambda b, j, be, ns: (layer, be[b], jj(b, j, ns), 0))],
            out_specs=pl.BlockSpec((tb, d), lambda b, j, be, ns: (b, 0), pipeline_mode=once),
            scratch_shapes=[pltpu.VMEM((tb, d), F32)]),
        out_shape=jax.ShapeDtypeStruct((p, d), BF16),
        compiler_params=_params(("arbitrary", "arbitrary"), MOE_VMEM_LIMIT),
        name="moe_experts",
    )(block_expert, block_nsub, xb, wg, wu, wd)


def _res_ln_kernel(x_ref, y1_ref, y2_ref, g1_ref, g2_ref, gate_ref, g_ref, b_ref, o_ref, *, alpha):
    f = g1_ref[...] * y1_ref[...].astype(F32) + g2_ref[...] * y2_ref[...].astype(F32)
    o_ref[...] = _layer_norm(alpha * x_ref[...] + (1.0 + gate_ref[...]) * f, g_ref[...], b_ref[...])


def _res_ln(x, y1, y2, g1, g2, gate, ln_g, ln_b, alpha):
    s, d = x.shape
    t = min(512, s)
    row = lambda i: (i, 0)
    vec = pl.BlockSpec((1, d), lambda i: (0, 0))
    col = pl.BlockSpec((t, 1), row)
    return pl.pallas_call(
        functools.partial(_res_ln_kernel, alpha=alpha),
        grid=(s // t,),
        in_specs=[pl.BlockSpec((t, d), row), pl.BlockSpec((t, d), row), pl.BlockSpec((t, d), row), col, col,
                  vec, vec, vec],
        out_specs=pl.BlockSpec((t, d), row),
        out_shape=jax.ShapeDtypeStruct((s, d), F32),
        compiler_params=_params(("parallel",)),
        name="res_ln",
    )(x, y1, y2, g1, g2, gate, ln_g.reshape(1, -1), ln_b.reshape(1, -1))


def _moe_ffn(x, h, sc, sh, w_router, wg, wu, wd, layer, gate, ln_g, ln_b, alpha):
    s, d = x.shape
    tb = min(MOE_BLOCK, s)
    route, counts = _router(x, sc, sh, w_router.T)
    e1 = route[0].astype(jnp.int32)
    e2 = route[1].astype(jnp.int32)
    counts = counts[:, 0].astype(jnp.int32)
    padded = (counts + tb - 1) // tb * tb
    pad_end = jnp.cumsum(padded)
    pad_start = pad_end - padded
    eids = jnp.arange(N_EXPERTS, dtype=jnp.int32)[:, None]
    lookup = lambda table, e: jnp.sum(jnp.where(e[None, :] == eids, table[:, None], 0), axis=0)
    d1 = lookup(pad_start, e1) + route[2].astype(jnp.int32)
    d2 = lookup(pad_start, e2) + route[3].astype(jnp.int32)
    n_blocks = (2 * s) // tb + N_EXPERTS
    p_tot = n_blocks * tb
    tok = jnp.arange(s, dtype=jnp.int32)
    tok_pad = jnp.zeros((p_tot,), jnp.int32).at[d1].set(tok).at[d2].set(tok)
    blk_start = jnp.arange(n_blocks, dtype=jnp.int32) * tb
    block_valid = blk_start < pad_end[-1]
    owner = jnp.minimum(jnp.sum((blk_start[None, :] >= pad_end[:, None]).astype(jnp.int32), axis=0),
                        N_EXPERTS - 1)
    last_expert = jnp.max(jnp.where(block_valid, owner, 0))
    block_expert = jnp.where(block_valid, owner, last_expert).astype(jnp.int32)
    sub = min(MOE_SUB, tb)
    rows_left = lookup(counts, block_expert) - (blk_start - lookup(pad_start, block_expert))
    block_nsub = jnp.where(block_valid, jnp.clip((rows_left + sub - 1) // sub, 0, tb // sub), 0)
    xb = jnp.take(h, tok_pad, axis=0)
    yb = _moe_experts(xb, block_expert, block_nsub.astype(jnp.int32), wg, wu, wd, layer, tb)
    return _res_ln(x, jnp.take(yb, d1, axis=0), jnp.take(yb, d2, axis=0), route[4][:, None], route[5][:, None],
                   gate, ln_g, ln_b, alpha)


def kernel(x, c, w_ada, b_ada, w_in, ssd_conv_w, ssd_conv_b, dt_bias, a_log, d_skip, ssd_norm_w, conf_conv_w, conf_conv_b, conf_ln_g, conf_ln_b, w_out, ln1_g, ln1_b, ln2_g, ln2_b, ffn_w_gate, ffn_w_up, ffn_w_down, moe_router, moe_w_gate, moe_w_up, moe_w_down):
    bsz, s, d = x.shape
    assert bsz == 1
    depth = w_ada.shape[0]
    alpha = (2 * depth) ** 0.25
    xs = x.reshape(s, d)

    mod = _ada_mod(c, w_ada, b_ada)
    z_end = SSD_WIDTH
    xbc_end = z_end + XBC_WIDTH
    dt_end = xbc_end + 2 * SSD_HEADS
    w_proj = jnp.concatenate(
        [w_in[:, :, z_end:dt_end].astype(BF16),
         jnp.zeros((depth, d, DT_PAD - 2 * SSD_HEADS), BF16),
         w_in[:, :, :z_end].astype(BF16),
         w_in[:, :, dt_end:].astype(BF16)], axis=2)
    w_out_b = w_out.astype(BF16)
    expand = jnp.repeat(jnp.eye(SSD_HEADS, dtype=BF16), SSD_HEAD_DIM, axis=1)
    d_x = jnp.repeat(d_skip, SSD_HEAD_DIM, axis=1)

    for l in range(depth):
        sh_m, sc_m, g_m, sh_f, sc_f, g_f = [mod[l, :, k * d:(k + 1) * d] for k in range(6)]
        proj = _mod_matmul(xs, sc_m, sh_m, w_proj[l])
        y_fwd, xbc = _ssd_fwd(proj, ssd_conv_w[l], ssd_conv_b[l], dt_bias[l, 0], a_log[l, 0], expand)
        y_ssd = _ssd_bwd(xbc, proj, y_fwd, dt_bias[l, 1], a_log[l, 1], expand, d_x[l:l + 1], ssd_norm_w[l])
        u = _conformer(proj, conf_conv_w[l], conf_conv_b[l], conf_ln_g[l], conf_ln_b[l])
        xs, h = _out_proj(y_ssd, u, w_out_b[l], xs, g_m, ln1_g[l], ln1_b[l], sc_f, sh_f, alpha)
        i = l // 2
        if l % 2 == 0:
            xs = _dense_ffn(h, ffn_w_gate[i].astype(BF16), ffn_w_up[i].astype(BF16),
                            ffn_w_down[i].astype(BF16), xs, g_f, ln2_g[l], ln2_b[l], alpha)
        else:
            xs = _moe_ffn(xs, h, sc_f, sh_f, moe_router[i], moe_w_gate, moe_w_up, moe_w_down, i, g_f,
                          ln2_g[l], ln2_b[l], alpha)
    return xs.reshape(bsz, s, d)
```

```python
import functools

import jax
import jax.numpy as jnp
from jax import lax
from jax.experimental import pallas as pl
from jax.experimental.pallas import tpu as pltpu

F32 = jnp.float32
BF16 = jnp.bfloat16

SSD_HEADS = 16
SSD_HEAD_DIM = 64
SSD_WIDTH = SSD_HEADS * SSD_HEAD_DIM
SSD_GROUPS = 2
SSD_STATE = 128
SSD_CHUNK = 128
SSD_CONV_WIDTH = 5
CONF_CH = 1024
CONF_CONV_WIDTH = 31
N_EXPERTS = 8
EPS = 1e-5

BC_WIDTH = 2 * SSD_GROUPS * SSD_STATE
XBC_WIDTH = SSD_WIDTH + BC_WIDTH
COL_DT = XBC_WIDTH
DT_PAD = 512
COL_Z = COL_DT + DT_PAD
COL_GLU = COL_Z + SSD_WIDTH
PROJ_COLS = COL_GLU + 2 * CONF_CH

LANE = 128
SUBLANE = 8
VMEM_LIMIT = 56 * 1024 * 1024


def _params(sem, vmem=VMEM_LIMIT):
    return pltpu.CompilerParams(dimension_semantics=sem, vmem_limit_bytes=vmem)


def _silu(v):
    return v * jax.nn.sigmoid(v)


def _softplus(v):
    return jnp.maximum(v, 0.0) + jnp.log1p(jnp.exp(-jnp.abs(v)))


def _layer_norm(v, g, b):
    mu = jnp.mean(v, axis=-1, keepdims=True)
    d = v - mu
    var = jnp.mean(d * d, axis=-1, keepdims=True)
    return d * lax.rsqrt(var + EPS) * g + b


def _split_bf16(v, terms):
    out = []
    for _ in range(terms):
        t = v.astype(BF16)
        out.append(t)
        v = v - t.astype(F32)
    return out


def _ada_kernel(cb_ref, w_ref, b_ref, o_ref):
    cb = cb_ref[...]
    act = _silu(cb)
    tn = w_ref.shape[-1]
    parts = [jnp.sum(w_ref[:, j * LANE:(j + 1) * LANE] * act, axis=0, keepdims=True)
             for j in range(tn // LANE)]
    o_ref[...] = jnp.concatenate(parts, axis=1) + b_ref[...]


def _ada_mod(c, w_ada, b_ada):
    depth, d, n = w_ada.shape
    tn = 1024
    cb = jnp.broadcast_to(c.reshape(d, 1), (d, LANE))
    return pl.pallas_call(
        _ada_kernel,
        grid=(depth, n // tn),
        in_specs=[pl.BlockSpec((d, LANE), lambda l, j: (0, 0)),
                  pl.BlockSpec((None, d, tn), lambda l, j: (l, 0, j)),
                  pl.BlockSpec((None, 1, tn), lambda l, j: (l, 0, j))],
        out_specs=pl.BlockSpec((None, 1, tn), lambda l, j: (l, 0, j)),
        out_shape=jax.ShapeDtypeStruct((depth, 1, n), F32),
        compiler_params=_params(("parallel", "parallel")),
        name="ada_mod",
    )(cb, w_ada, b_ada.reshape(depth, 1, n))


def _mod_matmul_kernel(x_ref, sc_ref, sh_ref, w_ref, o_ref, h_ref):
    @pl.when(pl.program_id(1) == 0)
    def _():
        h_ref[...] = (x_ref[...] * (1.0 + sc_ref[...]) + sh_ref[...]).astype(BF16)

    o_ref[...] = jnp.dot(h_ref[...], w_ref[...], preferred_element_type=F32)


def _mod_matmul(x, sc, sh, w):
    s, d = x.shape
    n = w.shape[1]
    tm = min(1024, s)
    tn = 1024
    return pl.pallas_call(
        _mod_matmul_kernel,
        grid=(s // tm, n // tn),
        in_specs=[pl.BlockSpec((tm, d), lambda i, j: (i, 0)),
                  pl.BlockSpec((1, d), lambda i, j: (0, 0)),
                  pl.BlockSpec((1, d), lambda i, j: (0, 0)),
                  pl.BlockSpec((d, tn), lambda i, j: (0, j))],
        out_specs=pl.BlockSpec((tm, tn), lambda i, j: (i, j)),
        out_shape=jax.ShapeDtypeStruct((s, n), F32),
        scratch_shapes=[pltpu.VMEM((tm, d), BF16)],
        compiler_params=_params(("parallel", "arbitrary")),
        name="in_proj",
    )(x, sc, sh, w)


CONV_ROWS = 32
CONV_COLS = 256


def _fill_halo(ext_ref, prev_ref, x_ref, next_ref, halo):
    i = pl.program_id(0)
    t = x_ref.shape[0]
    ext_ref[halo:halo + t, :] = x_ref[...]
    ext_ref[0:halo, :] = jnp.where(i > 0, prev_ref[...], 0.0)
    ext_ref[halo + t:halo + t + halo, :] = jnp.where(i < pl.num_programs(0) - 1, next_ref[...], 0.0)


def _halo_specs(t, halo, s, width, col):
    hb = t // halo
    last = s // halo - 1
    return [pl.BlockSpec((halo, width), lambda i: (jnp.maximum(i * hb - 1, 0), col)),
            pl.BlockSpec((t, width), lambda i: (i, col)),
            pl.BlockSpec((halo, width), lambda i: (jnp.minimum((i + 1) * hb, last), col))]


SSD_HALO = SUBLANE
SSD_BLOCK = 4 * SSD_CHUNK


def _ssd_chunk(xs, bc_ref, r0, dt_raw, dtb_ref, alog_ref, exp_ref, state_ref, reverse):
    L = SSD_CHUNK
    gw = SSD_WIDTH // SSD_GROUPS
    hpg = SSD_HEADS // SSD_GROUPS
    rows = slice(r0, r0 + L)

    row = lax.broadcasted_iota(jnp.int32, (L, L), 0)
    col = lax.broadcasted_iota(jnp.int32, (L, L), 1)
    mask = (row <= col) if reverse else (row >= col)
    tri = mask.astype(BF16)
    edge = 0 if reverse else L - 1

    dt = _softplus(dt_raw + dtb_ref[...])
    a = -jnp.exp(alog_ref[...])
    da_terms = _split_bf16(dt * a, 3)
    cum = sum(jnp.dot(tri, t, preferred_element_type=F32) for t in da_terms)
    cum_t = sum(lax.dot_general(t, tri, (((0,), (1,)), ((), ())), preferred_element_type=F32)
                for t in da_terms)
    total = cum[edge:edge + 1, :]
    ecum = jnp.exp(cum)
    stacked = jnp.concatenate([dt, ecum, dt * jnp.exp(total - cum)], axis=0)
    expand = exp_ref[...]
    wide = sum(jnp.dot(t, expand, preferred_element_type=F32) for t in _split_bf16(stacked, 2))
    dt_x, ecum_x, dts_x = wide[0:L], wide[L:2 * L], wide[2 * L:3 * L]
    cdecay_x = ecum_x[edge:edge + 1, :]

    xdt_b = (xs * dt_x).astype(BF16)
    xdt_s = (xs * dts_x).astype(BF16)

    halves = []
    for g in range(SSD_GROUPS):
        bg = bc_ref[rows, g * SSD_STATE:(g + 1) * SSD_STATE].astype(BF16)
        cg = bc_ref[rows, (SSD_GROUPS + g) * SSD_STATE:(SSD_GROUPS + g + 1) * SSD_STATE].astype(BF16)
        cb = lax.dot_general(cg, bg, (((1,), (1,)), ((), ())), preferred_element_type=F32)
        prev = state_ref[g]
        y_off = jnp.dot(cg, prev.astype(BF16), preferred_element_type=F32)
        y_off = y_off * ecum_x[:, g * gw:(g + 1) * gw]
        heads = []
        for r in range(hpg):
            h = g * hpg + r
            seg = cum[:, h:h + 1] - cum_t[h:h + 1, :]
            m = (cb * jnp.exp(jnp.where(mask, seg, -jnp.inf))).astype(BF16)
            heads.append(jnp.dot(m, xdt_b[:, h * SSD_HEAD_DIM:(h + 1) * SSD_HEAD_DIM],
                                 preferred_element_type=F32))
        halves.append(jnp.concatenate(heads, axis=1) + y_off)
        new = lax.dot_general(bg, xdt_s[:, g * gw:(g + 1) * gw], (((0,), (0,)), ((), ())),
                              preferred_element_type=F32)
        state_ref[g] = prev * cdecay_x[:, g * gw:(g + 1) * gw] + new
    return halves


def _ssd_fwd_kernel(prev_ref, cur_ref, next_ref, dt_ref, cw_ref, cbias_ref, dtb_ref, alog_ref, exp_ref,
                    y_ref, xbc_ref, ext_ref, state_ref):
    @pl.when(pl.program_id(0) == 0)
    def _():
        state_ref[...] = jnp.zeros_like(state_ref)

    _fill_halo(ext_ref, prev_ref, cur_ref, next_ref, SSD_HALO)
    t = cur_ref.shape[0]
    pad = (SSD_CONV_WIDTH - 1) // 2
    for r0 in range(0, t, CONV_ROWS):
        for c0 in range(0, XBC_WIDTH, CONV_COLS):
            acc = jnp.broadcast_to(cbias_ref[:, c0:c0 + CONV_COLS], (CONV_ROWS, CONV_COLS))
            for tap in range(SSD_CONV_WIDTH):
                start = SSD_HALO - pad + tap + r0
                acc = acc + cw_ref[tap:tap + 1, c0:c0 + CONV_COLS] * ext_ref[start:start + CONV_ROWS,
                                                                             c0:c0 + CONV_COLS]
            xbc_ref[r0:r0 + CONV_ROWS, c0:c0 + CONV_COLS] = _silu(acc)

    gw = SSD_WIDTH // SSD_GROUPS
    bc_ref = xbc_ref.at[:, SSD_WIDTH:XBC_WIDTH]
    for r0 in range(0, t, SSD_CHUNK):
        halves = _ssd_chunk(xbc_ref[r0:r0 + SSD_CHUNK, 0:SSD_WIDTH], bc_ref, r0,
                            dt_ref[r0:r0 + SSD_CHUNK, 0:SSD_HEADS], dtb_ref, alog_ref, exp_ref, state_ref,
                            reverse=False)
        for g in range(SSD_GROUPS):
            y_ref[r0:r0 + SSD_CHUNK, g * gw:(g + 1) * gw] = halves[g]


def _ssd_fwd(proj, conv_w, conv_b, dt_bias, a_log, expand):
    s = proj.shape[0]
    t = min(SSD_BLOCK, s)
    fix = lambda i: (0, 0)
    return pl.pallas_call(
        _ssd_fwd_kernel,
        grid=(s // t,),
        in_specs=_halo_specs(t, SSD_HALO, s, XBC_WIDTH, 0) + [
            pl.BlockSpec((t, LANE), lambda i: (i, COL_DT // LANE)),
            pl.BlockSpec((SSD_CONV_WIDTH, XBC_WIDTH), fix), pl.BlockSpec((1, XBC_WIDTH), fix),
            pl.BlockSpec((1, SSD_HEADS), fix), pl.BlockSpec((1, SSD_HEADS), fix),
            pl.BlockSpec((SSD_HEADS, SSD_WIDTH), fix)],
        out_specs=[pl.BlockSpec((t, SSD_WIDTH), lambda i: (i, 0)),
                   pl.BlockSpec((t, XBC_WIDTH), lambda i: (i, 0))],
        out_shape=[jax.ShapeDtypeStruct((s, SSD_WIDTH), F32), jax.ShapeDtypeStruct((s, XBC_WIDTH), F32)],
        scratch_shapes=[pltpu.VMEM((t + 2 * SSD_HALO, XBC_WIDTH), F32),
                        pltpu.VMEM((SSD_GROUPS, SSD_STATE, SSD_WIDTH // SSD_GROUPS), F32)],
        compiler_params=_params(("arbitrary",)),
        name="ssd_fwd",
    )(proj, proj, proj, proj, conv_w, conv_b.reshape(1, -1), dt_bias.reshape(1, -1), a_log.reshape(1, -1),
      expand)


def _ssd_bwd_kernel(xs_ref, bc_ref, dt_ref, z_ref, yf_ref, dtb_ref, alog_ref, exp_ref, dskip_ref, nw_ref,
                    o_ref, state_ref):
    @pl.when(pl.program_id(0) == 0)
    def _():
        state_ref[...] = jnp.zeros_like(state_ref)

    t = xs_ref.shape[0]
    for r0 in reversed(range(0, t, SSD_CHUNK)):
        rows = slice(r0, r0 + SSD_CHUNK)
        xs = xs_ref[rows, :]
        halves = _ssd_chunk(xs, bc_ref, r0, dt_ref[rows, SSD_HEADS:2 * SSD_HEADS], dtb_ref, alog_ref, exp_ref,
                            state_ref, reverse=True)
        y = jnp.concatenate(halves, axis=1) + yf_ref[rows, :] + dskip_ref[...] * xs
        v = y * _silu(z_ref[rows, :])
        ms = jnp.mean(v * v, axis=-1, keepdims=True)
        o_ref[rows, :] = (v * lax.rsqrt(ms + EPS) * nw_ref[...]).astype(o_ref.dtype)


def _ssd_bwd(xbc, proj, y_fwd, dt_bias, a_log, expand, d_x, norm_w):
    s = xbc.shape[0]
    t = min(SSD_BLOCK, s)
    nb = s // t
    rev = lambda col: (lambda i: (nb - 1 - i, col))
    fix = lambda i: (0, 0)
    vec = pl.BlockSpec((1, SSD_WIDTH), fix)
    return pl.pallas_call(
        _ssd_bwd_kernel,
        grid=(nb,),
        in_specs=[pl.BlockSpec((t, SSD_WIDTH), rev(0)),
                  pl.BlockSpec((t, BC_WIDTH), rev(SSD_WIDTH // BC_WIDTH)),
                  pl.BlockSpec((t, LANE), rev(COL_DT // LANE)),
                  pl.BlockSpec((t, SSD_WIDTH), rev(COL_Z // SSD_WIDTH)),
                  pl.BlockSpec((t, SSD_WIDTH), rev(0)),
                  pl.BlockSpec((1, SSD_HEADS), fix), pl.BlockSpec((1, SSD_HEADS), fix),
                  pl.BlockSpec((SSD_HEADS, SSD_WIDTH), fix), vec, vec],
        out_specs=pl.BlockSpec((t, SSD_WIDTH), rev(0)),
        out_shape=jax.ShapeDtypeStruct((s, SSD_WIDTH), BF16),
        scratch_shapes=[pltpu.VMEM((SSD_GROUPS, SSD_STATE, SSD_WIDTH // SSD_GROUPS), F32)],
        compiler_params=_params(("arbitrary",)),
        name="ssd_bwd",
    )(xbc, xbc, proj, proj, y_fwd, dt_bias.reshape(1, -1), a_log.reshape(1, -1), expand, d_x,
      norm_w.reshape(1, -1))


CONF_HALO = 16


def _conf_kernel(ap_ref, a_ref, an_ref, bp_ref, b_ref, bn_ref, w_ref, cb_ref, g_ref, beta_ref,
                 o_ref, ext_ref, gate_ref, shift_ref, conv_ref):
    _fill_halo(ext_ref, ap_ref, a_ref, an_ref, CONF_HALO)
    _fill_halo(gate_ref, bp_ref, b_ref, bn_ref, CONF_HALO)
    ext_ref[...] = ext_ref[...] * jax.nn.sigmoid(gate_ref[...])
    t, c = a_ref.shape
    span = shift_ref.shape[1]
    for ph in range(1, SUBLANE):
        shift_ref[ph - 1] = ext_ref[ph:ph + span, :]
    pad = (CONF_CONV_WIDTH - 1) // 2
    for r0 in range(0, t, CONV_ROWS):
        for c0 in range(0, c, CONV_COLS):
            acc = jnp.broadcast_to(cb_ref[:, c0:c0 + CONV_COLS], (CONV_ROWS, CONV_COLS))
            for tap in range(CONF_CONV_WIDTH):
                q, ph = divmod(CONF_HALO - pad + tap, SUBLANE)
                lo = q * SUBLANE + r0
                if ph == 0:
                    win = ext_ref[lo:lo + CONV_ROWS, c0:c0 + CONV_COLS]
                else:
                    win = shift_ref[ph - 1, lo:lo + CONV_ROWS, c0:c0 + CONV_COLS]
                acc = acc + w_ref[tap:tap + 1, c0:c0 + CONV_COLS] * win
            conv_ref[r0:r0 + CONV_ROWS, c0:c0 + CONV_COLS] = acc
    o_ref[...] = _silu(_layer_norm(conv_ref[...], g_ref[...], beta_ref[...])).astype(o_ref.dtype)


def _conformer(proj, conv_w, conv_b, ln_g, ln_b):
    s = proj.shape[0]
    t = min(512, s)
    ca = COL_GLU // CONF_CH
    fix = lambda i: (0, 0)
    vec = pl.BlockSpec((1, CONF_CH), fix)
    span = t + 2 * CONF_HALO - SUBLANE
    return pl.pallas_call(
        _conf_kernel,
        grid=(s // t,),
        in_specs=_halo_specs(t, CONF_HALO, s, CONF_CH, ca) + _halo_specs(t, CONF_HALO, s, CONF_CH, ca + 1) + [
            pl.BlockSpec((CONF_CONV_WIDTH, CONF_CH), fix), vec, vec, vec],
        out_specs=pl.BlockSpec((t, CONF_CH), lambda i: (i, 0)),
        out_shape=jax.ShapeDtypeStruct((s, CONF_CH), BF16),
        scratch_shapes=[pltpu.VMEM((t + 2 * CONF_HALO, CONF_CH), F32),
                        pltpu.VMEM((t + 2 * CONF_HALO, CONF_CH), F32),
                        pltpu.VMEM((SUBLANE - 1, span, CONF_CH), F32),
                        pltpu.VMEM((t, CONF_CH), F32)],
        compiler_params=_params(("parallel",)),
        name="conformer",
    )(proj, proj, proj, proj, proj, proj, conv_w, conv_b.reshape(1, -1), ln_g.reshape(1, -1),
      ln_b.reshape(1, -1))


def _out_proj_kernel(y_ref, u_ref, wy_ref, wu_ref, x_ref, gate_ref, g_ref, b_ref, sc_ref, sh_ref,
                     xo_ref, h_ref, *, alpha):
    half = x_ref.shape[0] // 2
    for r0 in (0, half):
        rows = slice(r0, r0 + half)
        mix = jnp.dot(y_ref[rows, :], wy_ref[...], preferred_element_type=F32)
        mix = mix + jnp.dot(u_ref[rows, :], wu_ref[...], preferred_element_type=F32)
        xn = _layer_norm(alpha * x_ref[rows, :] + (1.0 + gate_ref[...]) * mix, g_ref[...], b_ref[...])
        xo_ref[rows, :] = xn
        h_ref[rows, :] = (xn * (1.0 + sc_ref[...]) + sh_ref[...]).astype(h_ref.dtype)


def _out_proj(y_ssd, u, w_out, x, gate, ln_g, ln_b, sc, sh, alpha):
    s, d = x.shape
    tm = min(512, s)
    half = w_out.shape[0] // 2
    row = lambda i: (i, 0)
    fix = lambda i: (0, 0)
    vec = pl.BlockSpec((1, d), fix)
    return pl.pallas_call(
        functools.partial(_out_proj_kernel, alpha=alpha),
        grid=(s // tm,),
        in_specs=[pl.BlockSpec((tm, half), row), pl.BlockSpec((tm, half), row),
                  pl.BlockSpec((half, d), lambda i: (0, 0)), pl.BlockSpec((half, d), lambda i: (1, 0)),
                  pl.BlockSpec((tm, d), row), vec, vec, vec, vec, vec],
        out_specs=[pl.BlockSpec((tm, d), row), pl.BlockSpec((tm, d), row)],
        out_shape=[jax.ShapeDtypeStruct((s, d), F32), jax.ShapeDtypeStruct((s, d), BF16)],
        compiler_params=_params(("parallel",)),
        name="out_proj",
    )(y_ssd, u, w_out, w_out, x, gate, ln_g.reshape(1, -1), ln_b.reshape(1, -1), sc, sh)


FF_SPLIT = 2


def _swiglu_tile(h, wg_ref, wu_ref, wd_ref, cast=lambda w: w):
    tf = wg_ref.shape[-1]
    step = tf // FF_SPLIT
    out = None
    for c0 in range(0, tf, step):
        a = jnp.dot(h, cast(wg_ref[:, c0:c0 + step]), preferred_element_type=F32)
        b = jnp.dot(h, cast(wu_ref[:, c0:c0 + step]), preferred_element_type=F32)
        part = jnp.dot((_silu(a) * b).astype(BF16), cast(wd_ref[c0:c0 + step, :]), preferred_element_type=F32)
        out = part if out is None else out + part
    return out
def _ffn_kernel(h_ref, wg_ref, wu_ref, wd_ref, x_ref, gate_ref, g_ref, b_ref, o_ref, acc_ref, *, alpha):
    j = pl.program_id(1)

    @pl.when(j == 0)
    def _():
        acc_ref[...] = jnp.zeros_like(acc_ref)

    h = h_ref[...]
    acc_ref[...] += _swiglu_tile(h, wg_ref, wu_ref, wd_ref)

    @pl.when(j == pl.num_programs(1) - 1)
    def _():
        o_ref[...] = _layer_norm(alpha * x_ref[...] + (1.0 + gate_ref[...]) * acc_ref[...],
                                 g_ref[...], b_ref[...])


def _dense_ffn(h, wg, wu, wd, x, gate, ln_g, ln_b, alpha):
    s, d = x.shape
    f = wg.shape[1]
    tm = min(512, s)
    tf = 512
    row = lambda i, j: (i, 0)
    vec = pl.BlockSpec((1, d), lambda i, j: (0, 0))
    return pl.pallas_call(
        functools.partial(_ffn_kernel, alpha=alpha),
        grid=(s // tm, f // tf),
        in_specs=[pl.BlockSpec((tm, d), row),
                  pl.BlockSpec((d, tf), lambda i, j: (0, j)), pl.BlockSpec((d, tf), lambda i, j: (0, j)),
                  pl.BlockSpec((tf, d), lambda i, j: (j, 0)),
                  pl.BlockSpec((tm, d), row), vec, vec, vec],
        out_specs=pl.BlockSpec((tm, d), row),
        out_shape=jax.ShapeDtypeStruct((s, d), F32),
        scratch_shapes=[pltpu.VMEM((tm, d), F32)],
        compiler_params=_params(("parallel", "arbitrary")),
        name="dense_ffn",
    )(h, wg, wu, wd, x, gate, ln_g.reshape(1, -1), ln_b.reshape(1, -1))


def _router_kernel(x_ref, sc_ref, sh_ref, wr_ref, route_ref, cnt_ref, carry_ref):
    i = pl.program_id(0)
    tb = x_ref.shape[0]
    e = N_EXPERTS

    @pl.when(i == 0)
    def _():
        carry_ref[...] = jnp.zeros_like(carry_ref)

    h = x_ref[...] * (1.0 + sc_ref[...]) + sh_ref[...]
    logits = lax.dot_general(wr_ref[...], h, (((1,), (1,)), ((), ())), precision=lax.Precision.HIGHEST,
                             preferred_element_type=F32)
    eid = lax.broadcasted_iota(jnp.int32, (e, tb), 0).astype(F32)
    m1 = jnp.max(logits, axis=0, keepdims=True)
    i1 = jnp.min(jnp.where(logits == m1, eid, float(e)), axis=0, keepdims=True)
    rest = jnp.where(eid == i1, -jnp.inf, logits)
    m2 = jnp.max(rest, axis=0, keepdims=True)
    i2 = jnp.min(jnp.where(rest == m2, eid, float(e)), axis=0, keepdims=True)
    p2 = jnp.exp(m2 - m1)
    g1 = 1.0 / (1.0 + p2)
    g2 = p2 / (1.0 + p2)
    sel1 = eid == i1
    sel2 = eid == i2
    onehot = jnp.where(sel1 | sel2, 1.0, 0.0)
    r_i = lax.broadcasted_iota(jnp.int32, (tb, tb), 0)
    c_i = lax.broadcasted_iota(jnp.int32, (tb, tb), 1)
    before = (r_i < c_i).astype(BF16)
    rank = jnp.dot(onehot.astype(BF16), before, preferred_element_type=F32) + carry_ref[:, 0:1]
    r1 = jnp.sum(jnp.where(sel1, rank, 0.0), axis=0, keepdims=True)
    r2 = jnp.sum(jnp.where(sel2, rank, 0.0), axis=0, keepdims=True)
    zero = jnp.zeros_like(g1)
    route_ref[...] = jnp.concatenate(
        [i1, i2, r1, r2, g1, g2, zero, zero], axis=0)
    carry_ref[...] = carry_ref[...] + jnp.sum(onehot, axis=1, keepdims=True)
    cnt_ref[...] = carry_ref[...]


def _router(x, sc, sh, w_router_t):
    s, d = x.shape
    tb = min(512, s)
    fix = lambda i: (0, 0)
    return pl.pallas_call(
        _router_kernel,
        grid=(s // tb,),
        in_specs=[pl.BlockSpec((tb, d), lambda i: (i, 0)), pl.BlockSpec((1, d), fix),
                  pl.BlockSpec((1, d), fix), pl.BlockSpec((N_EXPERTS, d), fix)],
        out_specs=[pl.BlockSpec((8, tb), lambda i: (0, i)), pl.BlockSpec((N_EXPERTS, LANE), fix)],
        out_shape=[jax.ShapeDtypeStruct((8, s), F32), jax.ShapeDtypeStruct((N_EXPERTS, LANE), F32)],
        scratch_shapes=[pltpu.VMEM((N_EXPERTS, LANE), F32)],
        compiler_params=_params(("arbitrary",)),
        name="moe_router",
    )(x, sc, sh, w_router_t)


def _moe_kernel(be_ref, nsub_ref, xb_ref, wg_ref, wu_ref, wd_ref, o_ref, acc_ref, wbuf_ref, wdbuf_ref):
    b = pl.program_id(0)
    j = pl.program_id(1)
    last = pl.num_programs(1) - 1
    nsub = nsub_ref[b]

    @pl.when(j == 0)
    def _():
        acc_ref[...] = jnp.zeros_like(acc_ref)

    @pl.when(nsub > 0)
    def _():
        wbuf_ref[0] = wg_ref[...].astype(BF16)
        wbuf_ref[1] = wu_ref[...].astype(BF16)
        wdbuf_ref[...] = wd_ref[...].astype(BF16)
        for k in range(xb_ref.shape[0] // MOE_SUB):
            rows = slice(k * MOE_SUB, (k + 1) * MOE_SUB)

            @pl.when(nsub > k)
            def _():
                acc_ref[rows, :] += _swiglu_tile(xb_ref[rows, :], wbuf_ref.at[0], wbuf_ref.at[1], wdbuf_ref)

    @pl.when(j == last)
    def _():
        o_ref[...] = acc_ref[...].astype(o_ref.dtype)


MOE_BLOCK = 1024
MOE_SUB = 512
MOE_FF_TILE = 512
MOE_VMEM_LIMIT = 60 * 1024 * 1024


def _moe_experts(xb, block_expert, block_nsub, wg, wu, wd, layer, tb):
    p, d = xb.shape
    f = wg.shape[3]
    tf = MOE_FF_TILE
    nf = f // tf
    jj = lambda b, j, nsub: jnp.where(nsub[b] > 0, j, nf - 1)
    once = pl.Buffered(1)
    return pl.pallas_call(
        _moe_kernel,
        grid_spec=pltpu.PrefetchScalarGridSpec(
            num_scalar_prefetch=2,
            grid=(p // tb, nf),
            in_specs=[pl.BlockSpec((tb, d), lambda b, j, be, ns: (b, 0), pipeline_mode=once),
                      pl.BlockSpec((None, None, d, tf), lambda b, j, be, ns: (layer, be[b], 0, jj(b, j, ns))),
                      pl.BlockSpec((None, None, d, tf), lambda b, j, be, ns: (layer, be[b], 0, jj(b, j, ns))),
                      pl.BlockSpec((None, None, tf, d), lambda b, j, be, ns: (layer, be[b], jj(b, j, ns), 0))],
            out_specs=pl.BlockSpec((tb, d), lambda b, j, be, ns: (b, 0), pipeline_mode=once),
            scratch_shapes=[pltpu.VMEM((tb, d), F32), pltpu.VMEM((2, d, tf), BF16), pltpu.VMEM((tf, d), BF16)]),
        out_shape=jax.ShapeDtypeStruct((p, d), BF16),
        compiler_params=_params(("arbitrary", "arbitrary"), MOE_VMEM_LIMIT),
        name="moe_experts",
    )(block_expert, block_nsub, xb, wg, wu, wd)


def _res_ln_kernel(x_ref, y1_ref, y2_ref, route_ref, gate_ref, g_ref, b_ref, o_ref, *, alpha):
    r_i = lax.broadcasted_iota(jnp.int32, (8, 8), 0)
    c_i = lax.broadcasted_iota(jnp.int32, (8, 8), 1)
    eye = (r_i == c_i).astype(BF16)
    cols = sum(lax.dot_general(t, eye, (((0,), (0,)), ((), ())), preferred_element_type=F32)
               for t in _split_bf16(route_ref[...], 3))
    f = cols[:, 4:5] * y1_ref[...].astype(F32) + cols[:, 5:6] * y2_ref[...].astype(F32)
    o_ref[...] = _layer_norm(alpha * x_ref[...] + (1.0 + gate_ref[...]) * f, g_ref[...], b_ref[...])


def _res_ln(x, y1, y2, route, gate, ln_g, ln_b, alpha):
    s, d = x.shape
    t = min(512, s)
    row = lambda i: (i, 0)
    vec = pl.BlockSpec((1, d), lambda i: (0, 0))
    return pl.pallas_call(
        functools.partial(_res_ln_kernel, alpha=alpha),
        grid=(s // t,),
        in_specs=[pl.BlockSpec((t, d), row), pl.BlockSpec((t, d), row), pl.BlockSpec((t, d), row),
                  pl.BlockSpec((8, t), lambda i: (0, i)), vec, vec, vec],
        out_specs=pl.BlockSpec((t, d), row),
        out_shape=jax.ShapeDtypeStruct((s, d), F32),
        compiler_params=_params(("parallel",)),
        name="res_ln",
    )(x, y1, y2, route, gate, ln_g.reshape(1, -1), ln_b.reshape(1, -1))


def _plan_kernel(route_ref, cnt_ref, dest_ref, tab_ref, *, tb, sub):
    e = N_EXPERTS
    cnt = cnt_ref[...]
    padded = jnp.ceil(cnt * (1.0 / tb)) * tb
    run = jnp.zeros((1, LANE), F32)
    starts = []
    for k in range(e):
        starts.append(run)
        run = run + padded[k:k + 1]
    start = jnp.concatenate(starts, axis=0)
    total = run

    r = route_ref[...]
    t = r.shape[1]
    eid = lax.broadcasted_iota(jnp.int32, (e, t), 0).astype(F32)
    s_col = start[:, 0:1]
    d1 = jnp.sum(jnp.where(eid == r[0:1], s_col, 0.0), axis=0, keepdims=True) + r[2:3]
    d2 = jnp.sum(jnp.where(eid == r[1:2], s_col, 0.0), axis=0, keepdims=True) + r[3:4]
    zero = jnp.zeros((e - 2, t), F32)
    dest_ref[...] = jnp.concatenate([d1, d2, zero], axis=0).astype(jnp.int32)

    blk = lax.broadcasted_iota(jnp.int32, (1, LANE), 1).astype(F32) * tb
    owner = jnp.minimum(jnp.sum(jnp.where(blk >= start + padded, 1.0, 0.0), axis=0, keepdims=True), e - 1.0)
    valid = blk < total
    last_owner = jnp.max(jnp.where(valid, owner, 0.0), axis=1, keepdims=True)
    be = jnp.where(valid, owner, last_owner)
    eid8 = lax.broadcasted_iota(jnp.int32, (e, LANE), 0).astype(F32)
    mine = eid8 == be
    cnt_be = jnp.sum(jnp.where(mine, cnt, 0.0), axis=0, keepdims=True)
    start_be = jnp.sum(jnp.where(mine, start, 0.0), axis=0, keepdims=True)
    rows_left = cnt_be - (blk - start_be)
    nsub = jnp.where(valid, jnp.clip(jnp.ceil(rows_left * (1.0 / sub)), 0.0, float(tb // sub)), 0.0)
    zero8 = jnp.zeros((e - 2, LANE), F32)
    tab_ref[...] = jnp.concatenate([be, nsub, zero8], axis=0).astype(jnp.int32)


def _moe_plan(route, counts, tb, sub):
    s = route.shape[1]
    t = min(2048, s)
    return pl.pallas_call(
        functools.partial(_plan_kernel, tb=tb, sub=sub),
        grid=(s // t,),
        in_specs=[pl.BlockSpec((8, t), lambda i: (0, i)), pl.BlockSpec((N_EXPERTS, LANE), lambda i: (0, 0))],
        out_specs=[pl.BlockSpec((8, t), lambda i: (0, i)), pl.BlockSpec((8, LANE), lambda i: (0, 0))],
        out_shape=[jax.ShapeDtypeStruct((8, s), jnp.int32), jax.ShapeDtypeStruct((8, LANE), jnp.int32)],
        compiler_params=_params(("arbitrary",)),
        name="moe_plan",
    )(route, counts)


def _moe_ffn(x, h, sc, sh, w_router, wg, wu, wd, layer, gate, ln_g, ln_b, alpha):
    s, d = x.shape
    tb = min(MOE_BLOCK, s)
    sub = min(MOE_SUB, tb)
    n_blocks = (2 * s) // tb + N_EXPERTS
    assert n_blocks <= LANE
    p_tot = n_blocks * tb
    route, counts = _router(x, sc, sh, w_router.T)
    dest, tab = _moe_plan(route, counts, tb, sub)
    d1, d2 = dest[0], dest[1]
    tok = jnp.arange(s, dtype=jnp.int32)
    tok_pad = (jnp.arange(p_tot, dtype=jnp.int32) % s).at[d1].set(tok).at[d2].set(tok)
    xb = jnp.take(h, tok_pad, axis=0)
    yb = _moe_experts(xb, tab[0, :n_blocks], tab[1, :n_blocks], wg, wu, wd, layer, tb)
    return _res_ln(x, jnp.take(yb, d1, axis=0), jnp.take(yb, d2, axis=0), route, gate, ln_g, ln_b, alpha)


def kernel(x, c, w_ada, b_ada, w_in, ssd_conv_w, ssd_conv_b, dt_bias, a_log, d_skip, ssd_norm_w, conf_conv_w, conf_conv_b, conf_ln_g, conf_ln_b, w_out, ln1_g, ln1_b, ln2_g, ln2_b, ffn_w_gate, ffn_w_up, ffn_w_down, moe_router, moe_w_gate, moe_w_up, moe_w_down):
    bsz, s, d = x.shape
    assert bsz == 1
    depth = w_ada.shape[0]
    alpha = (2 * depth) ** 0.25
    xs = x.reshape(s, d)

    mod = _ada_mod(c, w_ada, b_ada)
    z_end = SSD_WIDTH
    xbc_end = z_end + XBC_WIDTH
    dt_end = xbc_end + 2 * SSD_HEADS
    w_proj = jnp.concatenate(
        [w_in[:, :, z_end:dt_end].astype(BF16),
         jnp.zeros((depth, d, DT_PAD - 2 * SSD_HEADS), BF16),
         w_in[:, :, :z_end].astype(BF16),
         w_in[:, :, dt_end:].astype(BF16)], axis=2)
    w_out_b = w_out.astype(BF16)
    expand = jnp.repeat(jnp.eye(SSD_HEADS, dtype=BF16), SSD_HEAD_DIM, axis=1)
    d_x = jnp.repeat(d_skip, SSD_HEAD_DIM, axis=1)

    for l in range(depth):
        sh_m, sc_m, g_m, sh_f, sc_f, g_f = [mod[l, :, k * d:(k + 1) * d] for k in range(6)]
        proj = _mod_matmul(xs, sc_m, sh_m, w_proj[l])
        y_fwd, xbc = _ssd_fwd(proj, ssd_conv_w[l], ssd_conv_b[l], dt_bias[l, 0], a_log[l, 0], expand)
        y_ssd = _ssd_bwd(xbc, proj, y_fwd, dt_bias[l, 1], a_log[l, 1], expand, d_x[l:l + 1], ssd_norm_w[l])
        u = _conformer(proj, conf_conv_w[l], conf_conv_b[l], conf_ln_g[l], conf_ln_b[l])
        xs, h = _out_proj(y_ssd, u, w_out_b[l], xs, g_m, ln1_g[l], ln1_b[l], sc_f, sh_f, alpha)
        i = l // 2
        if l % 2 == 0:
            xs = _dense_ffn(h, ffn_w_gate[i].astype(BF16), ffn_w_up[i].astype(BF16),
                            ffn_w_down[i].astype(BF16), xs, g_f, ln2_g[l], ln2_b[l], alpha)
        else:
            xs = _moe_ffn(xs, h, sc_f, sh_f, moe_router[i], moe_w_gate, moe_w_up, moe_w_down, i, g_f,
                          ln2_g[l], ln2_b[l], alpha)
    return xs.reshape(bsz, s, d)
```
